```python
import math
import jax
import jax.numpy as jnp
from jax import lax
import numpy as np

D_MODEL = 1024
BATCH = 32
SEQ = 2048
DEPTH = 4

GRID_W = 64
CTX_LEN = 256
HEAD_DIM = 64
MIX_GROUP_WIDTH = D_MODEL // 2
ROPE_THETA = 10000.0
BLOCK_Q = 128
NORM_EPS = 1e-6
NEG_INF = -1e30
MLP_HIDDEN = 4 * D_MODEL

A_HEADS = MIX_GROUP_WIDTH // HEAD_DIM
A_KV_HEADS = A_HEADS // 4
B_HEADS = MIX_GROUP_WIDTH // HEAD_DIM
B_WIDTH = B_HEADS * HEAD_DIM
B_DECAY_LORA = 64
B_AAA_LORA = 64
B_GATE_LORA = 128
B_GN_EPS = 64e-5
B_IN = 3 * B_WIDTH + 2 * B_DECAY_LORA + 2 * B_AAA_LORA + B_GATE_LORA
C_HEADS = MIX_GROUP_WIDTH // HEAD_DIM
C_KV_HEADS = C_HEADS // 4
WINDOW = 128
D_HEAD_DIM = 64
D_HEADS = MIX_GROUP_WIDTH // D_HEAD_DIM
D_INNER = D_HEADS * D_HEAD_DIM
D_GROUPS = 2
D_STATE = 128
D_CONV = 5
D_XBC = D_INNER + 2 * D_GROUPS * D_STATE
CHUNK = 128

N_EVEN = (DEPTH + 1) // 2
N_ODD = DEPTH // 2
AB_SIZES = (A_HEADS * HEAD_DIM, A_KV_HEADS * HEAD_DIM, A_KV_HEADS * HEAD_DIM, B_IN)
CD_SIZES = (C_HEADS * HEAD_DIM, C_KV_HEADS * HEAD_DIM, C_KV_HEADS * HEAD_DIM, D_INNER, D_XBC, 2 * D_HEADS)
AB_IN = sum(AB_SIZES)
CD_IN = sum(CD_SIZES)
F32 = jnp.float32

kernel_name = 'hybrid_dit_gqa_rwkv7_swa_ssd'


def rmsnorm(x, w):
    xf = x.astype(F32)
    y = xf * lax.rsqrt(jnp.mean(xf * xf, axis=-1, keepdims=True) + NORM_EPS)
    return (y * w.astype(F32)).astype(x.dtype)


def modulate(h, shift, scale):
    return h * (1.0 + scale) + shift


def sq_relu_mlp(h, w1, w2):
    return jnp.square(jax.nn.relu(h @ w1)) @ w2


def split_cols(t, sizes):
    return jnp.split(t, [int(s) for s in np.cumsum(sizes)[:-1]], axis=-1)


def heads(t, n_heads):
    return t.reshape(t.shape[0], t.shape[1], n_heads, HEAD_DIM)


def joint_softmax(parts):
    s = jnp.concatenate([p.astype(F32) for p in parts], axis=-1)
    prob = jax.nn.softmax(s, axis=-1)
    return jnp.split(prob, [int(v) for v in np.cumsum([p.shape[-1] for p in parts])[:-1]], axis=-1)


def axial_rope_tables(n_tokens):
    rows = n_tokens // GRID_W
    row = jnp.repeat(jnp.arange(rows, dtype=F32), GRID_W)
    col = jnp.tile(jnp.arange(GRID_W, dtype=F32), rows)
    n_freq = HEAD_DIM // 4
    inv_freq = ROPE_THETA ** (-jnp.arange(n_freq, dtype=F32) / n_freq)
    ang_r = row[:, None] * inv_freq
    ang_c = col[:, None] * inv_freq
    return (jnp.cos(ang_r), jnp.sin(ang_r), jnp.cos(ang_c), jnp.sin(ang_c))


def _rotate_half(x, cos, sin):
    x1, x2 = jnp.split(x, 2, axis=-1)
    return jnp.concatenate([x1 * cos - x2 * sin, x2 * cos + x1 * sin], axis=-1)


def apply_axial_rope(x, rope):
    cr, sr, cc, sc = (t[:, None, :].astype(x.dtype) for t in rope)
    xr, xc = jnp.split(x, 2, axis=-1)
    return jnp.concatenate([_rotate_half(xr, cr, sr), _rotate_half(xc, cc, sc)], axis=-1)


def global_gqa(q_l, k_l, v_l, q_c, k_c, v_c, q_norm, k_norm, rope, need_ctx):
    bsz, n_lat = q_l.shape[:2]
    rep = A_HEADS // A_KV_HEADS
    scale = HEAD_DIM ** -0.5
    ql = apply_axial_rope(rmsnorm(heads(q_l, A_HEADS), q_norm), rope)
    kl = apply_axial_rope(rmsnorm(heads(k_l, A_KV_HEADS), k_norm), rope)
    vl = heads(v_l, A_KV_HEADS)
    kc = rmsnorm(heads(k_c, A_KV_HEADS), k_norm)
    vc = heads(v_c, A_KV_HEADS)
    n_blk = n_lat // BLOCK_Q
    qb = jnp.moveaxis(ql.reshape(bsz, n_blk, BLOCK_Q, A_KV_HEADS, rep, HEAD_DIM), 1, 0)

    def block(q):
        s_lat = jnp.einsum('bqgrd,bkgd->bgrqk', q, kl).astype(F32) * scale
        s_ctx = jnp.einsum('bqgrd,bkgd->bgrqk', q, kc).astype(F32) * scale
        p_lat, p_ctx = joint_softmax([s_lat, s_ctx])
        return (jnp.einsum('bgrqk,bkgd->bqgrd', p_lat.astype(vl.dtype), vl)
                + jnp.einsum('bgrqk,bkgd->bqgrd', p_ctx.astype(vc.dtype), vc))

    o_l = jnp.moveaxis(lax.map(block, qb), 0, 1).reshape(bsz, n_lat, A_HEADS * HEAD_DIM)
    o_c = None
    if need_ctx:
        n_ctx = q_c.shape[1]
        qc = rmsnorm(heads(q_c, A_HEADS), q_norm).reshape(bsz, n_ctx, A_KV_HEADS, rep, HEAD_DIM)
        p = jax.nn.softmax(jnp.einsum('bqgrd,bkgd->bgrqk', qc, kc).astype(F32) * scale, axis=-1)
        o_c = jnp.einsum('bgrqk,bkgd->bqgrd', p.astype(vc.dtype), vc).reshape(bsz, n_ctx, A_HEADS * HEAD_DIM)
    return o_l, o_c


def window_gqa_sink(q_l, k_l, v_l, q_c, k_c, v_c, sink, rope, need_ctx):
    bsz, n_lat = q_l.shape[:2]
    rep = C_HEADS // C_KV_HEADS
    scale = HEAD_DIM ** -0.5
    ql = apply_axial_rope(heads(q_l, C_HEADS), rope)
    kl = apply_axial_rope(heads(k_l, C_KV_HEADS), rope)
    vl = heads(v_l, C_KV_HEADS)
    kc = heads(k_c, C_KV_HEADS)
    vc = heads(v_c, C_KV_HEADS)
    pad = ((0, 0), (WINDOW, WINDOW), (0, 0), (0, 0))
    kl_pad = jnp.pad(kl, pad)
    vl_pad = jnp.pad(vl, pad)
    span = BLOCK_Q + 2 * WINDOW
    rel = jnp.arange(span)[None, :] - jnp.arange(BLOCK_Q)[:, None]
    band = (rel >= 0) & (rel <= 2 * WINDOW)
    sink_logit = sink.astype(F32).reshape(1, C_KV_HEADS, rep, 1, 1)
    n_blk = n_lat // BLOCK_Q
    qb = jnp.moveaxis(ql.reshape(bsz, n_blk, BLOCK_Q, C_KV_HEADS, rep, HEAD_DIM), 1, 0)

    def block(args):
        i, q = args
        start = i * BLOCK_Q
        kb = lax.dynamic_slice_in_dim(kl_pad, start, span, axis=1)
        vb = lax.dynamic_slice_in_dim(vl_pad, start, span, axis=1)
        kpos = start - WINDOW + jnp.arange(span)
        valid = band & ((kpos >= 0) & (kpos < n_lat))[None, :]
        s_lat = jnp.where(valid, jnp.einsum('bqgrd,bkgd->bgrqk', q, kb).astype(F32) * scale, NEG_INF)
        s_ctx = jnp.einsum('bqgrd,bkgd->bgrqk', q, kc).astype(F32) * scale
        s_sink = jnp.broadcast_to(sink_logit, s_ctx.shape[:-1] + (1,))
        p_lat, p_ctx, _ = joint_softmax([s_lat, s_ctx, s_sink])
        return (jnp.einsum('bgrqk,bkgd->bqgrd', p_lat.astype(vb.dtype), vb)
                + jnp.einsum('bgrqk,bkgd->bqgrd', p_ctx.astype(vc.dtype), vc))

    o_l = jnp.moveaxis(lax.map(block, (jnp.arange(n_blk), qb)), 0, 1).reshape(bsz, n_lat, C_HEADS * HEAD_DIM)
    o_c = None
    if need_ctx:
        n_ctx = q_c.shape[1]
        qc = heads(q_c, C_HEADS).reshape(bsz, n_ctx, C_KV_HEADS, rep, HEAD_DIM)
        s_ctx = jnp.einsum('bqgrd,bkgd->bgrqk', qc, kc).astype(F32) * scale
        s_sink = jnp.broadcast_to(sink_logit, s_ctx.shape[:-1] + (1,))
        p_ctx, _ = joint_softmax([s_ctx, s_sink])
        o_c = jnp.einsum('bgrqk,bkgd->bqgrd', p_ctx.astype(vc.dtype), vc).reshape(bsz, n_ctx, C_HEADS * HEAD_DIM)
    return o_l, o_c


def token_shift_centred(f, mu_prev, mu_next):
    prev = jnp.pad(f[:, :-1], ((0, 0), (1, 0), (0, 0)))
    nxt = jnp.pad(f[:, 1:], ((0, 0), (0, 1), (0, 0)))
    return f + mu_prev * (prev - f) + mu_next * (nxt - f)


def rwkv7_scan(state0, r, w, k, v, kk, a, reverse):
    def step(S, inp):
        r_t, w_t, k_t, v_t, kk_t, a_t = inp
        sa = jnp.einsum('bhvk,bhk->bhv', S, kk_t)
        S = (S * w_t[:, :, None, :] - sa[..., None] * (kk_t * a_t)[:, :, None, :]
             + v_t[..., None] * k_t[:, :, None, :])
        return S, jnp.einsum('bhvk,bhk->bhv', S, r_t)
    xs = tuple(jnp.moveaxis(t, 1, 0) for t in (r, w, k, v, kk, a))
    s_fin, y = lax.scan(step, state0, xs, reverse=reverse)
    return jnp.moveaxis(y, 0, 1), s_fin


def rwkv7_bidir(f_l, f_c, mu_prev, mu_next, w0, w2, a0, a2, g2, k_k, k_a, r_k, ln_w, ln_b, need_ctx):
    def prep(f):
        bsz, T = f.shape[:2]
        f = token_shift_centred(f, mu_prev, mu_next)
        r, k, v, wd, ad, gd = split_cols(f, (B_WIDTH, B_WIDTH, B_WIDTH, 2 * B_DECAY_LORA, 2 * B_AAA_LORA, B_GATE_LORA))
        wd = wd.reshape(bsz, T, 2, B_DECAY_LORA)
        ad = ad.reshape(bsz, T, 2, B_AAA_LORA)
        wlog = (w0 + jnp.einsum('btdl,dlc->btdc', jnp.tanh(wd), w2)).astype(F32)
        decay = jnp.exp(-jnp.exp(-jax.nn.softplus(-wlog) - 0.5))
        a = jax.nn.sigmoid((a0 + jnp.einsum('btdl,dlc->btdc', ad, a2)).astype(F32))
        kk = (k * k_k).astype(F32).reshape(bsz, T, B_HEADS, HEAD_DIM)
        kk = kk * lax.rsqrt(jnp.sum(kk * kk, axis=-1, keepdims=True) + 1e-12)
        k_dir = k.astype(F32)[:, :, None] * (1.0 + (a - 1.0) * k_a.astype(F32))
        g = jax.nn.sigmoid(gd) @ g2
        hd2 = lambda t: t.reshape(bsz, T, 2, B_HEADS, HEAD_DIM)
        return (r.astype(F32).reshape(bsz, T, B_HEADS, HEAD_DIM), v.astype(F32).reshape(bsz, T, B_HEADS, HEAD_DIM),
                kk, hd2(decay), hd2(a), hd2(k_dir), g)

    rc, vc, kkc, wc, ac, kc, gc = prep(f_c)
    rl, vl, kkl, wl, al, kl, gl = prep(f_l)
    bsz = f_l.shape[0]
    zero = jnp.zeros((bsz, B_HEADS, HEAD_DIM, HEAD_DIM), F32)
    y_c, y_l = [], []
    for d, rev in ((0, False), (1, True)):
        yc_d, s_ctx = rwkv7_scan(zero, rc, wc[:, :, d], kc[:, :, d], vc, kkc, ac[:, :, d], rev)
        yl_d, _ = rwkv7_scan(s_ctx, rl, wl[:, :, d], kl[:, :, d], vl, kkl, al[:, :, d], rev)
        y_c.append(yc_d)
        y_l.append(yl_d)

    def post(y, r, k_dir, v, g):
        mu = jnp.mean(y, axis=-1, keepdims=True)
        var = jnp.mean(jnp.square(y - mu), axis=-1, keepdims=True)
        y = ((y - mu) * lax.rsqrt(var + B_GN_EPS) * ln_w.astype(F32).reshape(B_HEADS, HEAD_DIM)
             + ln_b.astype(F32).reshape(B_HEADS, HEAD_DIM))
        bonus = jnp.sum(r[:, :, None] * k_dir * r_k.astype(F32), axis=-1, keepdims=True)
        y = y + jnp.sum(bonus, axis=2) * v
        return y.reshape(y.shape[0], y.shape[1], B_WIDTH) * g

    o_l = post(y_l[0] + y_l[1], rl, kl, vl, gl)
    o_c = post(y_c[0] + y_c[1], rc, kc, vc, gc) if need_ctx else None
    return o_l, o_c


def depthwise_conv_centred(u, w, b):
    out = lax.conv_general_dilated(u, w[:, None, :].astype(u.dtype), window_strides=(1,),
                                   padding=((D_CONV // 2, D_CONV // 2),),
                                   dimension_numbers=('NWC', 'WIO', 'NWC'), feature_group_count=u.shape[-1])
    return out + b


def ssd_chunked(x, dt, A, Bm, Cm, state0):
    bsz, T = x.shape[:2]
    nc = T // CHUNK
    R = D_HEADS // D_GROUPS
    xc = x.astype(F32).reshape(bsz, nc, CHUNK, D_GROUPS, R, D_HEAD_DIM)
    Bc = Bm.astype(F32).reshape(bsz, nc, CHUNK, D_GROUPS, D_STATE)
    Cc = Cm.astype(F32).reshape(bsz, nc, CHUNK, D_GROUPS, D_STATE)
    dtc = dt.reshape(bsz, nc, CHUNK, D_GROUPS, R)
    acs = jnp.cumsum(dtc * A.astype(F32).reshape(D_GROUPS, R), axis=2)
    acs_t = jnp.moveaxis(acs, 2, -1)
    seg = acs_t[..., :, None] - acs_t[..., None, :]
    lower = jnp.tril(jnp.ones((CHUNK, CHUNK), dtype=bool))
    Lmat = jnp.exp(jnp.where(lower, seg, -jnp.inf))
    CB = jnp.einsum('bcign,bcjgn->bcgij', Cc, Bc)
    M = CB[:, :, :, None] * Lmat * jnp.moveaxis(dtc, 2, -1)[..., None, :]
    y_diag = jnp.einsum('bcgrij,bcjgrp->bcigrp', M, xc)
    decay_to_end = jnp.exp(acs[:, :, -1:] - acs)
    chunk_states = jnp.einsum('bcjgn,bcjgr,bcjgrp->bcgrpn', Bc, decay_to_end * dtc, xc)
    chunk_decay = jnp.exp(acs[:, :, -1])

    def carry(h, inp):
        st, dec = inp
        return h * dec[..., None, None] + st, h

    h0 = state0.reshape(bsz, D_GROUPS, R, D_HEAD_DIM, D_STATE)
    h_fin, h_prev = lax.scan(carry, h0, (jnp.moveaxis(chunk_states, 1, 0), jnp.moveaxis(chunk_decay, 1, 0)))
    h_prev = jnp.moveaxis(h_prev, 0, 1)
    y_off = jnp.einsum('bcign,bcgrpn->bcigrp', Cc, h_prev) * jnp.exp(acs)[..., None]
    y = (y_diag + y_off).reshape(bsz, T, D_HEADS, D_HEAD_DIM)
    return y, h_fin.reshape(bsz, D_HEADS, D_HEAD_DIM, D_STATE)


def gated_group_rmsnorm(y, z, w):
    bsz, T = y.shape[:2]
    gs = D_INNER // D_GROUPS
    u = y.reshape(bsz, T, D_GROUPS, gs) * jax.nn.silu(z.astype(F32)).reshape(bsz, T, D_GROUPS, gs)
    u = u * lax.rsqrt(jnp.mean(u * u, axis=-1, keepdims=True) + NORM_EPS)
    return u.reshape(bsz, T, D_INNER) * w.astype(F32)


def ssd_bidir(z_l, xbc_l, dtr_l, z_c, xbc_c, dtr_c, conv_w, conv_b, dt_bias, A_log, D_skip, norm_w, need_ctx):
    def prep(xbc, dt_raw):
        bsz, T = xbc.shape[:2]
        u = jax.nn.silu(depthwise_conv_centred(xbc, conv_w, conv_b))
        xs, bm, cm = split_cols(u, (D_INNER, D_GROUPS * D_STATE, D_GROUPS * D_STATE))
        dt = jax.nn.softplus(dt_raw.astype(F32).reshape(bsz, T, 2, D_HEADS) + dt_bias.astype(F32))
        return (xs.reshape(bsz, T, D_HEADS, D_HEAD_DIM), bm.reshape(bsz, T, D_GROUPS, D_STATE),
                cm.reshape(bsz, T, D_GROUPS, D_STATE), dt)

    A = -jnp.exp(A_log.astype(F32))
    xs_c, b_c, c_c, dt_c = prep(xbc_c, dtr_c)
    xs_l, b_l, c_l, dt_l = prep(xbc_l, dtr_l)
    bsz = xbc_l.shape[0]
    zero = jnp.zeros((bsz, D_HEADS, D_HEAD_DIM, D_STATE), F32)
    fl = lambda t: jnp.flip(t, axis=1)
    yc_f, s_f = ssd_chunked(xs_c, dt_c[:, :, 0], A[0], b_c, c_c, zero)
    yl_f, _ = ssd_chunked(xs_l, dt_l[:, :, 0], A[0], b_l, c_l, s_f)
    yc_b, s_b = ssd_chunked(fl(xs_c), fl(dt_c[:, :, 1]), A[1], fl(b_c), fl(c_c), zero)
    yl_b, _ = ssd_chunked(fl(xs_l), fl(dt_l[:, :, 1]), A[1], fl(b_l), fl(c_l), s_b)
    skip = D_skip.astype(F32)[:, None]
    o_l = gated_group_rmsnorm(yl_f + fl(yl_b) + skip * xs_l.astype(F32), z_l, norm_w)
    o_c = gated_group_rmsnorm(yc_f + fl(yc_b) + skip * xs_c.astype(F32), z_c, norm_w) if need_ctx else None
    return o_l, o_c


def attn_rwkv_mixers(h_lat, h_ctx, w_in, q_norm, k_norm, mu_prev, mu_next, w0, w2, a0, a2, g2,
                     k_k, k_a, r_k, ln_w, ln_b, rope, need_ctx):
    q_l, k_l, v_l, f_l = split_cols(h_lat @ w_in, AB_SIZES)
    q_c, k_c, v_c, f_c = split_cols(h_ctx @ w_in, AB_SIZES)
    a_l, a_c = global_gqa(q_l, k_l, v_l, q_c, k_c, v_c, q_norm, k_norm, rope, need_ctx)
    b_l, b_c = rwkv7_bidir(f_l, f_c, mu_prev, mu_next, w0, w2, a0, a2, g2, k_k, k_a, r_k, ln_w, ln_b, need_ctx)
    dt = h_lat.dtype
    o_l = jnp.concatenate([a_l.astype(dt), b_l.astype(dt)], axis=-1)
    o_c = jnp.concatenate([a_c.astype(dt), b_c.astype(dt)], axis=-1) if need_ctx else None
    return o_l, o_c


def swa_ssd_mixers(h_lat, h_ctx, w_in, sink, conv_w, conv_b, dt_bias, A_log, D_skip, norm_w, rope, need_ctx):
    q_l, k_l, v_l, z_l, xbc_l, dtr_l = split_cols(h_lat @ w_in, CD_SIZES)
    q_c, k_c, v_c, z_c, xbc_c, dtr_c = split_cols(h_ctx @ w_in, CD_SIZES)
    c_l, c_c = window_gqa_sink(q_l, k_l, v_l, q_c, k_c, v_c, sink, rope, need_ctx)
    d_l, d_c = ssd_bidir(z_l, xbc_l, dtr_l, z_c, xbc_c, dtr_c, conv_w, conv_b, dt_bias, A_log, D_skip, norm_w, need_ctx)
    dt = h_lat.dtype
    o_l = jnp.concatenate([c_l.astype(dt), d_l.astype(dt)], axis=-1)
    o_c = jnp.concatenate([c_c.astype(dt), d_c.astype(dt)], axis=-1) if need_ctx else None
    return o_l, o_c


def setup_inputs(seed: int = 0) -> dict:
    key = jax.random.key(seed)
    keys = jax.random.split(key, 40)
    counter = [0]

    def nxt():
        k = keys[counter[0]]
        counter[0] += 1
        return k

    def nrm(shape, scale):
        return jax.random.normal(nxt(), shape, F32) * scale

    def unif(shape, lo, hi):
        return jax.random.uniform(nxt(), shape, F32, lo, hi)

    D = D_MODEL
    NE, NO = N_EVEN, N_ODD
    out = {
        'x': nrm((BATCH, SEQ, D), 1.0),
        'c': nrm((BATCH, D), 1.0),
        'ctx': nrm((BATCH, CTX_LEN, D), 1.0),
        'c_ctx': nrm((D,), 1.0),
        'w_mod': nrm((DEPTH, D, 6 * D), 0.5 * D ** -0.5),
        'b_mod': nrm((DEPTH, 6 * D), 0.02),
        'norm_mix': 1.0 + nrm((DEPTH, D), 0.05),
        'norm_mlp': 1.0 + nrm((DEPTH, D), 0.05),
        'w_out': nrm((DEPTH, D, D), D ** -0.5),
        'mlp_w1': nrm((DEPTH, D, MLP_HIDDEN), D ** -0.5),
        'mlp_w2': nrm((DEPTH, MLP_HIDDEN, D), MLP_HIDDEN ** -0.5),
        'final_norm': 1.0 + nrm((D,), 0.05),
        'ab_w_in': nrm((NE, D, AB_IN), D ** -0.5),
        'a_q_norm': 1.0 + nrm((NE, HEAD_DIM), 0.05),
        'a_k_norm': 1.0 + nrm((NE, HEAD_DIM), 0.05),
        'b_mu_prev': unif((NE, B_IN), 0.0, 0.4),
        'b_mu_next': unif((NE, B_IN), 0.0, 0.4),
        'b_w0': unif((NE, 2, B_WIDTH), -4.0, 1.0),
        'b_w2': nrm((NE, 2, B_DECAY_LORA, B_WIDTH), 0.1),
        'b_a0': nrm((NE, 2, B_WIDTH), 0.5),
        'b_a2': nrm((NE, 2, B_AAA_LORA, B_WIDTH), 0.1),
        'b_g2': nrm((NE, B_GATE_LORA, B_WIDTH), B_GATE_LORA ** -0.5),
        'b_k_k': 0.85 + nrm((NE, B_WIDTH), 0.05),
        'b_k_a': 1.0 + nrm((NE, B_WIDTH), 0.05),
        'b_r_k': nrm((NE, B_HEADS, HEAD_DIM), 0.1),
        'b_ln_w': 1.0 + nrm((NE, B_WIDTH), 0.05),
        'b_ln_b': nrm((NE, B_WIDTH), 0.02),
        'cd_w_in': nrm((NO, D, CD_IN), D ** -0.5),
        'c_sink': nrm((NO, C_HEADS), 0.5),
        'd_conv_w': nrm((NO, D_CONV, D_XBC), D_CONV ** -0.5),
        'd_conv_b': nrm((NO, D_XBC), 0.02),
    }
    dt_init = jnp.exp(unif((NO, 2, D_HEADS), math.log(1e-3), math.log(1e-1)))
    out['d_dt_bias'] = dt_init + jnp.log(-jnp.expm1(-dt_init))
    out['d_A_log'] = jnp.log(unif((NO, 2, D_HEADS), 1.0, 16.0))
    out['d_D'] = 1.0 + nrm((NO, D_HEADS), 0.05)
    out['d_norm_w'] = 1.0 + nrm((NO, D_INNER), 0.05)
    return out


def reference(x, c, ctx, c_ctx, w_mod, b_mod, norm_mix, norm_mlp, w_out, mlp_w1, mlp_w2, final_norm,
              ab_w_in, a_q_norm, a_k_norm, b_mu_prev, b_mu_next, b_w0, b_w2, b_a0, b_a2, b_g2,
              b_k_k, b_k_a, b_r_k, b_ln_w, b_ln_b,
              cd_w_in, c_sink, d_conv_w, d_conv_b, d_dt_bias, d_A_log, d_D, d_norm_w):
    n_lat = x.shape[1]
    rope = axial_rope_tables(n_lat)
    x_lat, x_ctx = x, ctx
    for i in range(DEPTH):
        need_ctx = i < DEPTH - 1
        j = i // 2
        mod_lat = jnp.split((jax.nn.silu(c) @ w_mod[i] + b_mod[i])[:, None, :], 6, axis=-1)
        mod_ctx = jnp.split(jax.nn.silu(c_ctx) @ w_mod[i] + b_mod[i], 6, axis=-1)
        h_lat = modulate(rmsnorm(x_lat, norm_mix[i]), mod_lat[0], mod_lat[1])
        h_ctx = modulate(rmsnorm(x_ctx, norm_mix[i]), mod_ctx[0], mod_ctx[1])
        if i % 2 == 0:
            o_lat, o_ctx = attn_rwkv_mixers(h_lat, h_ctx, ab_w_in[j], a_q_norm[j], a_k_norm[j], b_mu_prev[j],
                                            b_mu_next[j], b_w0[j], b_w2[j], b_a0[j], b_a2[j], b_g2[j], b_k_k[j],
                                            b_k_a[j], b_r_k[j], b_ln_w[j], b_ln_b[j], rope, need_ctx)
        else:
            o_lat, o_ctx = swa_ssd_mixers(h_lat, h_ctx, cd_w_in[j], c_sink[j], d_conv_w[j], d_conv_b[j],
                                          d_dt_bias[j], d_A_log[j], d_D[j], d_norm_w[j], rope, need_ctx)
        x_lat = x_lat + mod_lat[2] * (o_lat @ w_out[i])
        h2 = modulate(rmsnorm(x_lat, norm_mlp[i]), mod_lat[3], mod_lat[4])
        x_lat = x_lat + mod_lat[5] * sq_relu_mlp(h2, mlp_w1[i], mlp_w2[i])
        if need_ctx:
            x_ctx = x_ctx + mod_ctx[2] * (o_ctx @ w_out[i])
            h2c = modulate(rmsnorm(x_ctx, norm_mlp[i]), mod_ctx[3], mod_ctx[4])
            x_ctx = x_ctx + mod_ctx[5] * sq_relu_mlp(h2c, mlp_w1[i], mlp_w2[i])
    return rmsnorm(x_lat, final_norm)
```

```python
import functools
import math

import numpy as np
import jax
import jax.numpy as jnp
from jax import lax
from jax.experimental import pallas as pl
from jax.experimental.pallas import tpu as pltpu

F32 = jnp.float32
BF16 = jnp.bfloat16

D_MODEL = 1024
DEPTH = 4
GRID_W = 64
HEAD_DIM = 64
ROPE_THETA = 10000.0
NORM_EPS = 1e-6
MLP_HIDDEN = 4 * D_MODEL
N_HEADS = 8
MIX_W = N_HEADS * HEAD_DIM
KV_HEADS = 2
KV_DUP_W = 2 * KV_HEADS * HEAD_DIM
B_LORA_W = 128
B_IN = 3 * MIX_W + 3 * B_LORA_W
B_GN_EPS = 64e-5
WINDOW = 128
D_STATE = 128
D_GROUPS = 2
D_CONV = 5
D_XBC = MIX_W + 2 * D_GROUPS * D_STATE
DT_PAD = 128
SSD_CHUNK = 128
RWKV_CHUNK = 64
ATT_TQ = 128
HALO = 8
VMEM_LIMIT = 56 * 1024 * 1024


def _cparams(sem):
    return pltpu.CompilerParams(dimension_semantics=sem, vmem_limit_bytes=VMEM_LIMIT)


def _dot(a, b):
    return jnp.dot(a.astype(BF16), b.astype(BF16), preferred_element_type=F32)


def _dot_nt(a, b):
    return lax.dot_general(a.astype(BF16), b.astype(BF16), (((1,), (1,)), ((), ())),
                           preferred_element_type=F32)


def _split_hi_lo(x):
    hi = x.astype(BF16)
    lo = (x - hi.astype(F32)).astype(BF16)
    return hi, lo


def _dot_exact_rhs(x, m):
    hi, lo = _split_hi_lo(x)
    return (jnp.dot(hi, m, preferred_element_type=F32)
            + jnp.dot(lo, m, preferred_element_type=F32))


def _dot_exact_lhs(m, x):
    hi, lo = _split_hi_lo(x)
    return (jnp.dot(m, hi, preferred_element_type=F32)
            + jnp.dot(m, lo, preferred_element_type=F32))


def _rope(x, cos, sin_signed):
    w = x.shape[-1]
    lane = lax.broadcasted_iota(jnp.int32, x.shape, 1)
    partner = jnp.where((lane & 16) == 0, pltpu.roll(x, w - 16, 1), pltpu.roll(x, 16, 1))
    return x * cos + partner * sin_signed


def _shift_rows(cur, prev8, next8, s):
    n = cur.shape[0]
    if s == 0:
        return cur
    ri = lax.broadcasted_iota(jnp.int32, (HALO, cur.shape[1]), 0)
    if s > 0:
        rolled = pltpu.roll(cur, s, 0)
        fill = pltpu.roll(prev8, s, 0)
        head = jnp.where(ri < s, fill, rolled[:HALO])
        return jnp.concatenate([head, rolled[HALO:]], axis=0)
    a = -s
    rolled = pltpu.roll(cur, n - a, 0)
    fill = pltpu.roll(next8, HALO - a, 0)
    tail = jnp.where(ri >= HALO - a, fill, rolled[n - HALO:])
    return jnp.concatenate([rolled[:n - HALO], tail], axis=0)


def _sigmoid(x):
    return 1.0 / (1.0 + jnp.exp(-x))


def _softplus(x):
    return jnp.maximum(x, 0.0) + jnp.log(1.0 + jnp.exp(-jnp.abs(x)))


def _mod_kernel(c_ref, w_ref, b_ref, o_ref):
    c = c_ref[...]
    o_ref[0] = _dot(c * _sigmoid(c), w_ref[0]) + b_ref[0]


def _mod_call(c_all, w_mod, b_mod):
    rows = c_all.shape[0]
    n_col = w_mod.shape[2] // D_MODEL
    return pl.pallas_call(
        _mod_kernel,
        grid=(DEPTH, n_col),
        in_specs=[pl.BlockSpec((rows, D_MODEL), lambda i, j: (0, 0)),
                  pl.BlockSpec((1, D_MODEL, D_MODEL), lambda i, j: (i, 0, j)),
                  pl.BlockSpec((1, 1, D_MODEL), lambda i, j: (i, 0, j))],
        out_specs=pl.BlockSpec((1, rows, D_MODEL), lambda i, j: (i, 0, j)),
        out_shape=jax.ShapeDtypeStruct((DEPTH, rows, w_mod.shape[2]), F32),
        compiler_params=_cparams(("arbitrary", "arbitrary")),
        name="modulation",
    )(c_all, w_mod, b_mod.reshape(DEPTH, 1, -1))


def _mod_spec(n_batch, ctx_tiles, tile_off):
    return pl.BlockSpec((1, 6, D_MODEL),
                        lambda b, t: (jnp.where(t + tile_off < ctx_tiles, n_batch, b), 0, 0))


def _head_rms(y, seg):
    w = y.shape[-1]
    ssum = _dot_exact_rhs(y * y, seg[:w, :w])
    return y * lax.rsqrt(ssum * (1.0 / HEAD_DIM) + NORM_EPS)


def _premix_kernel(x_ref, mod_ref, nw_ref, w_ref, cos_ref, sin_ref, qnw_ref, knw_ref, seg_ref,
                   *out_refs, widths, qk_norm):
    x = x_ref[0]
    ms = jnp.mean(x * x, axis=-1, keepdims=True)
    h = x * lax.rsqrt(ms + NORM_EPS) * nw_ref[...]
    h = h * (1.0 + mod_ref[0, 1:2, :]) + mod_ref[0, 0:1, :]
    hb = h.astype(BF16)
    off = 0
    for idx, (o_ref, wdt) in enumerate(zip(out_refs, widths)):
        y = jnp.dot(hb, w_ref[:, off:off + wdt], preferred_element_type=F32)
        if idx < 2:
            if qk_norm:
                nw = qnw_ref if idx == 0 else knw_ref
                y = _head_rms(y, seg_ref[...]) * nw[...]
            y = _rope(y, cos_ref[:, :wdt], sin_ref[:, :wdt])
            if idx == 0:
                y = y * (HEAD_DIM ** -0.5)
        o_ref[0] = y.astype(o_ref.dtype)
        off += wdt


def _premix_call(x_all, mod, norm_w, w_in, cos, sin, qnw, knw, seg, widths, dtypes, qk_norm,
                 n_batch, ctx_tiles, tm):
    bsz, t_all, _ = x_all.shape
    nt = t_all // tm
    n_in = w_in.shape[1]
    const = lambda b, t: (0, 0)
    tok = lambda b, t: (b, t, 0)
    return pl.pallas_call(
        functools.partial(_premix_kernel, widths=widths, qk_norm=qk_norm),
        grid=(bsz, nt),
        in_specs=[pl.BlockSpec((1, tm, D_MODEL), tok),
                  _mod_spec(n_batch, ctx_tiles, 0),
                  pl.BlockSpec((1, D_MODEL), const),
                  pl.BlockSpec((D_MODEL, n_in), const),
                  pl.BlockSpec((tm, MIX_W), lambda b, t: (t, 0)),
                  pl.BlockSpec((tm, MIX_W), lambda b, t: (t, 0)),
                  pl.BlockSpec((1, MIX_W), const),
                  pl.BlockSpec((1, KV_DUP_W), const),
                  pl.BlockSpec((MIX_W, MIX_W), const)],
        out_specs=[pl.BlockSpec((1, tm, w), tok) for w in widths],
        out_shape=[jax.ShapeDtypeStruct((bsz, t_all, w), d) for w, d in zip(widths, dtypes)],
        compiler_params=_cparams(("parallel", "arbitrary")),
        name="premix_qknorm" if qk_norm else "premix",
    )(x_all, mod, norm_w, w_in, cos, sin, qnw, knw, seg)


def _stack_heads(q_ref, g, tq):
    lane = lax.broadcasted_iota(jnp.int32, (tq, 2 * HEAD_DIM), 1)
    parts = []
    for r in range(4):
        h = 4 * g + r
        slab = q_ref[0, :, 128 * (h // 2):128 * (h // 2) + 128]
        keep = (lane < HEAD_DIM) if h % 2 == 0 else (lane >= HEAD_DIM)
        parts.append(jnp.where(keep, slab, jnp.zeros_like(slab)))
    return jnp.concatenate(parts, axis=0)


def _store_heads(o_ref, o, g, tq):
    lane = lax.broadcasted_iota(jnp.int32, (tq, 2 * HEAD_DIM), 1)
    for j in range(2):
        even = o[(2 * j) * tq:(2 * j + 1) * tq]
        odd = o[(2 * j + 1) * tq:(2 * j + 2) * tq]
        col = 128 * (2 * g + j)
        o_ref[0, :, col:col + 128] = jnp.where(lane < HEAD_DIM, even, odd).astype(o_ref.dtype)


def _gqa_kernel(q_ref, k_ref, v_ref, o_ref, *, ctx_len, ctx_tiles):
    tq = q_ref.shape[1]
    t_all = k_ref.shape[1]

    def attend(n_keys):
        for g in range(KV_HEADS):
            qs = _stack_heads(q_ref, g, tq)
            kd = k_ref[0, :n_keys, 128 * g:128 * g + 128]
            vd = v_ref[0, :n_keys, 128 * g:128 * g + 128]
            s = _dot_nt(qs, kd)
            m = jnp.max(s, axis=-1, keepdims=True)
            p = jnp.exp(s - m)
            l = jnp.sum(p, axis=-1, keepdims=True)
            o = _dot(p, vd) / l
            _store_heads(o_ref, o, g, tq)

    t = pl.program_id(1)

    @pl.when(t < ctx_tiles)
    def _():
        attend(ctx_len)

    @pl.when(t >= ctx_tiles)
    def _():
        attend(t_all)


def _gqa_call(q, kdup, vdup, ctx_len):
    bsz, t_all, _ = q.shape
    tq = ATT_TQ
    ctx_tiles = ctx_len // tq
    return pl.pallas_call(
        functools.partial(_gqa_kernel, ctx_len=ctx_len, ctx_tiles=ctx_tiles),
        grid=(bsz, t_all // tq),
        in_specs=[pl.BlockSpec((1, tq, MIX_W), lambda b, t: (b, t, 0)),
                  pl.BlockSpec((1, t_all, KV_DUP_W), lambda b, t: (b, 0, 0)),
                  pl.BlockSpec((1, t_all, KV_DUP_W), lambda b, t: (b, 0, 0))],
        out_specs=pl.BlockSpec((1, tq, MIX_W), lambda b, t: (b, t, 0)),
        out_shape=jax.ShapeDtypeStruct((bsz, t_all, MIX_W), BF16),
        compiler_params=_cparams(("parallel", "arbitrary")),
        name="global_gqa",
    )(q, kdup, vdup)


def _swa_kernel(sink_ref, q_ref, kc_ref, vc_ref, kp_ref, kq_ref, kn_ref, vp_ref, vq_ref, vn_ref,
                o_ref, *, ctx_tiles, tile_off, n_tiles_all):
    tq = q_ref.shape[1]
    t = pl.program_id(1) + tile_off
    ri = lax.broadcasted_iota(jnp.int32, (4 * tq, tq), 0) & (tq - 1)
    ci = lax.broadcasted_iota(jnp.int32, (4 * tq, tq), 1)

    def sink_col(g):
        return jnp.concatenate([jnp.full((tq, 1), sink_ref[4 * g + r], F32) for r in range(4)], axis=0)

    def finish(g, m, parts):
        sk = sink_col(g)
        m = jnp.maximum(m, sk)
        l = jnp.exp(sk - m)
        o = None
        for s, vd in parts:
            p = jnp.exp(s - m)
            l = l + jnp.sum(p, axis=-1, keepdims=True)
            pv = _dot(p, vd)
            o = pv if o is None else o + pv
        _store_heads(o_ref, o / l, g, tq)

    @pl.when(t < ctx_tiles)
    def _():
        for g in range(KV_HEADS):
            qs = _stack_heads(q_ref, g, tq)
            s = _dot_nt(qs, kc_ref[0, :, 128 * g:128 * g + 128])
            finish(g, jnp.max(s, axis=-1, keepdims=True), [(s, vc_ref[0, :, 128 * g:128 * g + 128])])

    @pl.when(t >= ctx_tiles)
    def _():
        lim_prev = jnp.where(t > ctx_tiles, 0, tq)
        lim_next = jnp.where(t < n_tiles_all - 1, 0, tq)
        neg = jnp.float32(-1e30)
        for g in range(KV_HEADS):
            qs = _stack_heads(q_ref, g, tq)
            sl = slice(128 * g, 128 * g + 128)
            s_c = _dot_nt(qs, kc_ref[0, :, sl])
            s_p = jnp.where(ci - ri >= lim_prev, _dot_nt(qs, kp_ref[0, :, sl]), neg)
            s_q = _dot_nt(qs, kq_ref[0, :, sl])
            s_n = jnp.where(ri - ci >= lim_next, _dot_nt(qs, kn_ref[0, :, sl]), neg)
            m = jnp.maximum(jnp.max(s_c, axis=-1, keepdims=True), jnp.max(s_q, axis=-1, keepdims=True))
            m = jnp.maximum(m, jnp.max(s_p, axis=-1, keepdims=True))
            m = jnp.maximum(m, jnp.max(s_n, axis=-1, keepdims=True))
            finish(g, m, [(s_c, vc_ref[0, :, sl]), (s_p, vp_ref[0, :, sl]),
                          (s_q, vq_ref[0, :, sl]), (s_n, vn_ref[0, :, sl])])


def _swa_call(sink, q, kdup, vdup, ctx_len, lat_only):
    bsz, t_all, _ = q.shape
    tq = ATT_TQ
    ctx_tiles = ctx_len // tq
    n_all = t_all // tq
    off = ctx_tiles if lat_only else 0
    cur = lambda b, t: (b, t + off, 0)
    prv = lambda b, t: (b, jnp.maximum(t + off - 1, ctx_tiles), 0)
    nxt = lambda b, t: (b, jnp.minimum(t + off + 1, n_all - 1), 0)
    ctx = lambda b, t: (b, 0, 0)
    kv = lambda im: pl.BlockSpec((1, tq, KV_DUP_W), im)
    return pl.pallas_call(
        functools.partial(_swa_kernel, ctx_tiles=ctx_tiles, tile_off=off, n_tiles_all=n_all),
        grid=(bsz, n_all - off),
        in_specs=[pl.BlockSpec(memory_space=pltpu.SMEM),
                  pl.BlockSpec((1, tq, MIX_W), cur),
                  pl.BlockSpec((1, ctx_len, KV_DUP_W), ctx),
                  pl.BlockSpec((1, ctx_len, KV_DUP_W), ctx),
                  kv(prv), kv(cur), kv(nxt), kv(prv), kv(cur), kv(nxt)],
        out_specs=pl.BlockSpec((1, tq, MIX_W), lambda b, t: (b, t, 0)),
        out_shape=jax.ShapeDtypeStruct((bsz, t_all - off * tq, MIX_W), BF16),
        compiler_params=_cparams(("parallel", "arbitrary")),
        name="window_gqa",
    )(sink, q, kdup, vdup, kdup, kdup, kdup, vdup, vdup, vdup)


def _segment_flags(t, ctx_tiles, n_tiles):
    first = (t == 0) | (t == ctx_tiles)
    last = (t == ctx_tiles - 1) | (t == n_tiles - 1)
    return first, last


def _rwkv_prep_kernel(f_ref, fp_ref, fn_ref, mup_ref, mun_ref, w0_ref, w2_ref, a0_ref, a2_ref, g2_ref,
                      kk_w_ref, ka_ref, rk_ref, seg_ref,
                      r_ref, v_ref, kk_ref, lw_ref, bb_ref, kd_ref, bonus_ref, g_ref,
                      *, ctx_tiles, n_tiles):
    first, last = _segment_flags(pl.program_id(1), ctx_tiles, n_tiles)
    cur = f_ref[0]
    prev8 = jnp.where(first, 0.0, fp_ref[0])
    next8 = jnp.where(last, 0.0, fn_ref[0])
    prv = _shift_rows(cur, prev8, next8, 1)
    nxt = _shift_rows(cur, prev8, next8, -1)
    f = cur + mup_ref[...] * (prv - cur) + mun_ref[...] * (nxt - cur)
    r = f[:, 0:MIX_W]
    k = f[:, MIX_W:2 * MIX_W]
    v = f[:, 2 * MIX_W:3 * MIX_W]
    wd = f[:, 3 * MIX_W:3 * MIX_W + B_LORA_W]
    ad = f[:, 3 * MIX_W + B_LORA_W:3 * MIX_W + 2 * B_LORA_W]
    gd = f[:, 3 * MIX_W + 2 * B_LORA_W:]
    wlog = w0_ref[...] + _dot(jnp.tanh(wd), w2_ref[...])
    lw_ref[0] = -math.exp(-0.5) * _sigmoid(wlog)
    a = _sigmoid(a0_ref[...] + _dot(ad, a2_ref[...]))
    kk = k * kk_w_ref[...]
    kk = kk * lax.rsqrt(_dot_exact_rhs(kk * kk, seg_ref[...]) + 1e-12)
    ksum = jnp.zeros_like(k)
    for d in range(2):
        a_d = a[:, d * MIX_W:(d + 1) * MIX_W]
        k_d = k * (1.0 + (a_d - 1.0) * ka_ref[...])
        bb_ref[0, :, d * MIX_W:(d + 1) * MIX_W] = a_d * kk
        kd_ref[0, :, d * MIX_W:(d + 1) * MIX_W] = k_d
        ksum = ksum + k_d
    r_ref[0] = r
    v_ref[0] = v
    kk_ref[0] = kk
    bonus_ref[0] = _dot_exact_rhs(r * ksum * rk_ref[...], seg_ref[...]) * v
    g_ref[0] = _dot(_sigmoid(gd), g2_ref[...])


def _rwkv_prep_call(f, p, ctx_len, tm):
    bsz, t_all, _ = f.shape
    nt = t_all // tm
    ctx_tiles = ctx_len // tm
    nb8 = t_all // HALO
    const = lambda b, t: (0, 0)
    tok = lambda b, t: (b, t, 0)
    wide = jax.ShapeDtypeStruct((bsz, t_all, 2 * MIX_W), F32)
    one = jax.ShapeDtypeStruct((bsz, t_all, MIX_W), F32)
    row = lambda w: pl.BlockSpec((1, w), const)
    return pl.pallas_call(
        functools.partial(_rwkv_prep_kernel, ctx_tiles=ctx_tiles, n_tiles=nt),
        grid=(bsz, nt),
        in_specs=[pl.BlockSpec((1, tm, B_IN), tok),
                  pl.BlockSpec((1, HALO, B_IN), lambda b, t: (b, jnp.maximum(t * (tm // HALO) - 1, 0), 0)),
                  pl.BlockSpec((1, HALO, B_IN), lambda b, t: (b, jnp.minimum((t + 1) * (tm // HALO), nb8 - 1), 0)),
                  row(B_IN), row(B_IN),
                  row(2 * MIX_W), pl.BlockSpec((B_LORA_W, 2 * MIX_W), const),
                  row(2 * MIX_W), pl.BlockSpec((B_LORA_W, 2 * MIX_W), const),
                  pl.BlockSpec((B_LORA_W, MIX_W), const),
                  row(MIX_W), row(MIX_W), row(MIX_W),
                  pl.BlockSpec((MIX_W, MIX_W), const)],
        out_specs=[pl.BlockSpec((1, tm, MIX_W), tok)] * 3 + [pl.BlockSpec((1, tm, 2 * MIX_W), tok)] * 3
                  + [pl.BlockSpec((1, tm, MIX_W), tok)] * 2,
        out_shape=[one, one, one, wide, wide, wide, one, one],
        compiler_params=_cparams(("parallel", "arbitrary")),
        name="rwkv_prep",
    )(f, f, f, p["mu_prev"], p["mu_next"], p["w0"], p["w2"], p["a0"], p["a2"], p["g2"],
      p["k_k"], p["k_a"], p["r_k"], p["seg"])


def _scan_chunk(s, d, n_ctx, n_all):
    bwd = jnp.where(s < n_ctx, n_ctx - 1 - s, n_all - 1 + n_ctx - s)
    return jnp.where(d == 0, s, bwd)


def _rwkv_scan_kernel(r_ref, v_ref, kk_ref, lw_ref, bb_ref, kd_ref, y_ref, state_ref):
    c = r_ref.shape[1]
    d = pl.program_id(1)

    @pl.when(pl.program_id(2) == 0)
    def _():
        state_ref[...] = jnp.zeros_like(state_ref)

    ri = lax.broadcasted_iota(jnp.int32, (c, c), 0)
    ci = lax.broadcasted_iota(jnp.int32, (c, c), 1)
    ahead = jnp.where(d == 0, ri - ci, ci - ri)
    incl = ahead >= 0
    strict = ahead > 0
    eye = jnp.where(ri == ci, 1.0, 0.0)

    lw = lw_ref[0]
    cum = _dot_exact_lhs(jnp.where(incl, 1.0, 0.0).astype(BF16), lw)
    tot = jnp.sum(lw, axis=0, keepdims=True)
    e_cum = jnp.exp(cum)
    e_inv = jnp.exp(-cum)
    qt = r_ref[0] * e_cum
    kkt = kk_ref[0] * jnp.exp(cum - lw)
    kh = kd_ref[0] * e_inv
    bh = bb_ref[0] * e_inv
    e_tot = jnp.exp(tot)
    v = v_ref[0]
    ys = []
    for h in range(N_HEADS):
        sl = slice(HEAD_DIM * h, HEAD_DIM * (h + 1))
        lhs = jnp.concatenate([qt[:, sl], kkt[:, sl]], axis=0).astype(BF16)
        rhs = jnp.concatenate([kh[:, sl], bh[:, sl]], axis=0).astype(BF16)
        a = _dot_nt(lhs, rhs)
        a_rk = jnp.where(incl, a[:c, :c], 0.0)
        a_rb = jnp.where(incl, a[:c, c:], 0.0)
        a_kk = jnp.where(strict, a[c:, :c], 0.0)
        a_kb = jnp.where(strict, a[c:, c:], 0.0)
        pw = -a_kb
        tinv = eye + pw
        m = 1
        while 2 * m < c:
            pw = _dot(pw, pw)
            tinv = tinv + _dot(tinv, pw)
            m *= 2
        st = state_ref[h]
        x1 = _dot_nt(lhs, st)
        v_h = v[:, sl]
        u = _dot(tinv, x1[c:] + _dot(a_kk, v_h))
        ys.append(x1[:c] + _dot(a_rk, v_h) - _dot(a_rb, u))
        e_tot_h = e_tot[:, sl]
        wv = jnp.concatenate([v_h, u], axis=0)
        xk = jnp.concatenate([kh[:, sl] * e_tot_h, -(bh[:, sl] * e_tot_h)], axis=0)
        state_ref[h] = st * e_tot_h + _dot(wv.T, xk)
    y_ref[0, 0] = jnp.concatenate(ys, axis=1)


def _rwkv_scan_call(r, v, kk, lw, bb, kd, ctx_len):
    bsz, t_all, _ = r.shape
    c = RWKV_CHUNK
    n_all = t_all // c
    n_ctx = ctx_len // c
    one = lambda b, d, s: (b, _scan_chunk(s, d, n_ctx, n_all), 0)
    two = lambda b, d, s: (b, _scan_chunk(s, d, n_ctx, n_all), d)
    blk = lambda im: pl.BlockSpec((1, c, MIX_W), im)
    return pl.pallas_call(
        _rwkv_scan_kernel,
        grid=(bsz, 2, n_all),
        in_specs=[blk(one), blk(one), blk(one), blk(two), blk(two), blk(two)],
        out_specs=pl.BlockSpec((1, 1, c, MIX_W), lambda b, d, s: (d, b, _scan_chunk(s, d, n_ctx, n_all), 0)),
        out_shape=jax.ShapeDtypeStruct((2, bsz, t_all, MIX_W), F32),
        scratch_shapes=[pltpu.VMEM((N_HEADS, HEAD_DIM, HEAD_DIM), F32)],
        compiler_params=_cparams(("parallel", "arbitrary", "arbitrary")),
        name="rwkv_scan",
    )(r, v, kk, lw, bb, kd)


def _conv_kernel(x_ref, xp_ref, xn_ref, w_ref, b_ref, o_ref, *, ctx_tiles, n_tiles):
    first, last = _segment_flags(pl.program_id(1), ctx_tiles, n_tiles)
    cur = x_ref[0]
    prev8 = jnp.where(first, 0.0, xp_ref[0])
    next8 = jnp.where(last, 0.0, xn_ref[0])
    acc = b_ref[...] + w_ref[D_CONV // 2:D_CONV // 2 + 1, :] * cur
    for kpos in range(D_CONV):
        s = D_CONV // 2 - kpos
        if s != 0:
            acc = acc + w_ref[kpos:kpos + 1, :] * _shift_rows(cur, prev8, next8, s)
    o_ref[0] = acc * _sigmoid(acc)


def _conv_call(xbc, conv_w, conv_b, ctx_len, tm):
    bsz, t_all, w = xbc.shape
    nt = t_all // tm
    nb8 = t_all // HALO
    return pl.pallas_call(
        functools.partial(_conv_kernel, ctx_tiles=ctx_len // tm, n_tiles=nt),
        grid=(bsz, nt),
        in_specs=[pl.BlockSpec((1, tm, w), lambda b, t: (b, t, 0)),
                  pl.BlockSpec((1, HALO, w), lambda b, t: (b, jnp.maximum(t * (tm // HALO) - 1, 0), 0)),
                  pl.BlockSpec((1, HALO, w), lambda b, t: (b, jnp.minimum((t + 1) * (tm // HALO), nb8 - 1), 0)),
                  pl.BlockSpec((D_CONV, w), lambda b, t: (0, 0)),
                  pl.BlockSpec((1, w), lambda b, t: (0, 0))],
        out_specs=pl.BlockSpec((1, tm, w), lambda b, t: (b, t, 0)),
        out_shape=jax.ShapeDtypeStruct((bsz, t_all, w), F32),
        compiler_params=_cparams(("parallel", "arbitrary")),
        name="ssd_conv",
    )(xbc, xbc, xbc, conv_w, conv_b)


def _ssd_scan_kernel(u_ref, dt_ref, dtb_ref, a_ref, sel_ref, exp_ref, y_ref, state_ref):
    c = u_ref.shape[1]
    d = pl.program_id(1)

    @pl.when(pl.program_id(2) == 0)
    def _():
        state_ref[...] = jnp.zeros_like(state_ref)

    ri = lax.broadcasted_iota(jnp.int32, (c, c), 0)
    ci = lax.broadcasted_iota(jnp.int32, (c, c), 1)
    incl = jnp.where(d == 0, ri - ci, ci - ri) >= 0
    incl_m = jnp.where(incl, 1.0, 0.0).astype(BF16)

    dt_all = _softplus(dt_ref[0] + dtb_ref[...])
    dt_d = _dot_exact_rhs(dt_all, sel_ref[0])
    dta_d = _dot_exact_rhs(dt_all * a_ref[...], sel_ref[0])
    cum_d = _dot_exact_lhs(incl_m, dta_d)
    cum_t = cum_d.T
    dt_t = dt_d.T
    expand = exp_ref[...]
    dta_x = _dot_exact_rhs(dta_d, expand)
    cum_x = _dot_exact_lhs(incl_m, dta_x)
    dt_x = _dot_exact_rhs(dt_d, expand)
    tot_x = jnp.sum(dta_x, axis=0, keepdims=True)

    xs = u_ref[0, :, 0:MIX_W]
    gw = D_STATE
    ys = []
    for g in range(D_GROUPS):
        bm = u_ref[0, :, MIX_W + gw * g:MIX_W + gw * (g + 1)]
        cm = u_ref[0, :, MIX_W + gw * D_GROUPS + gw * g:MIX_W + gw * D_GROUPS + gw * (g + 1)]
        cb = _dot_nt(cm, bm)
        for r in range(N_HEADS // D_GROUPS):
            h = g * (N_HEADS // D_GROUPS) + r
            seg = cum_d[:, h:h + 1] - cum_t[h:h + 1, :]
            lmat = jnp.exp(jnp.where(incl, seg, -jnp.inf))
            mm = cb * lmat * dt_t[h:h + 1, :]
            ys.append(_dot(mm, xs[:, HEAD_DIM * h:HEAD_DIM * (h + 1)]))
    y_diag = jnp.concatenate(ys, axis=1)

    gl = MIX_W // D_GROUPS
    wj = jnp.exp(tot_x - cum_x) * dt_x
    xw = xs * wj
    e_cum = jnp.exp(cum_x)
    e_tot = jnp.exp(tot_x)
    offs = []
    for g in range(D_GROUPS):
        bm = u_ref[0, :, MIX_W + gw * g:MIX_W + gw * (g + 1)]
        cm = u_ref[0, :, MIX_W + gw * D_GROUPS + gw * g:MIX_W + gw * D_GROUPS + gw * (g + 1)]
        st = state_ref[:, gl * g:gl * (g + 1)]
        offs.append(_dot(cm, st))
        state_ref[:, gl * g:gl * (g + 1)] = st * e_tot[:, gl * g:gl * (g + 1)] + _dot(bm.T, xw[:, gl * g:gl * (g + 1)])
    y_ref[0, 0] = y_diag + jnp.concatenate(offs, axis=1) * e_cum


def _ssd_scan_call(u, dt_raw, dtb, a_row, sel, expand, ctx_len):
    bsz, t_all, w = u.shape
    c = SSD_CHUNK
    n_all = t_all // c
    n_ctx = ctx_len // c
    tok = lambda b, d, s: (b, _scan_chunk(s, d, n_ctx, n_all), 0)
    const = lambda b, d, s: (0, 0)
    return pl.pallas_call(
        _ssd_scan_kernel,
        grid=(bsz, 2, n_all),
        in_specs=[pl.BlockSpec((1, c, w), tok),
                  pl.BlockSpec((1, c, DT_PAD), tok),
                  pl.BlockSpec((1, DT_PAD), const),
                  pl.BlockSpec((1, DT_PAD), const),
                  pl.BlockSpec((1, DT_PAD, DT_PAD), lambda b, d, s: (d, 0, 0)),
                  pl.BlockSpec((DT_PAD, MIX_W), const)],
        out_specs=pl.BlockSpec((1, 1, c, MIX_W), lambda b, d, s: (d, b, _scan_chunk(s, d, n_ctx, n_all), 0)),
        out_shape=jax.ShapeDtypeStruct((2, bsz, t_all, MIX_W), F32),
        scratch_shapes=[pltpu.VMEM((D_STATE, MIX_W), F32)],
        compiler_params=_cparams(("parallel", "arbitrary", "arbitrary")),
        name="ssd_scan",
    )(u, dt_raw, dtb, a_row, sel, expand)


def _out_proj(x_ref, mod_ref, att_ref, mix, wo_ref, o_ref):
    out = (jnp.dot(att_ref[0], wo_ref[0:MIX_W, :], preferred_element_type=F32)
           + jnp.dot(mix.astype(BF16), wo_ref[MIX_W:, :], preferred_element_type=F32))
    o_ref[0] = x_ref[0] + mod_ref[0, 2:3, :] * out


def _post_rwkv_kernel(x_ref, mod_ref, att_ref, yf_ref, yb_ref, bonus_ref, g_ref, lnw_ref, lnb_ref,
                      seg_ref, wo_ref, o_ref):
    y = yf_ref[0, 0] + yb_ref[0, 0]
    inv = 1.0 / HEAD_DIM
    mu = _dot_exact_rhs(y, seg_ref[...]) * inv
    yc = y - mu
    var = _dot_exact_rhs(yc * yc, seg_ref[...]) * inv
    yn = yc * lax.rsqrt(var + B_GN_EPS) * lnw_ref[...] + lnb_ref[...]
    _out_proj(x_ref, mod_ref, att_ref, (yn + bonus_ref[0]) * g_ref[0], wo_ref, o_ref)


def _post_ssd_kernel(x_ref, mod_ref, att_ref, yf_ref, yb_ref, u_ref, z_ref, dsk_ref, nw_ref,
                     wo_ref, o_ref):
    z = z_ref[0]
    y = (yf_ref[0, 0] + yb_ref[0, 0] + dsk_ref[...] * u_ref[0]) * (z * _sigmoid(z))
    gl = MIX_W // D_GROUPS
    parts = []
    for g in range(D_GROUPS):
        yg = y[:, gl * g:gl * (g + 1)]
        parts.append(yg * lax.rsqrt(jnp.mean(yg * yg, axis=-1, keepdims=True) + NORM_EPS))
    _out_proj(x_ref, mod_ref, att_ref, jnp.concatenate(parts, axis=1) * nw_ref[...], wo_ref, o_ref)


def _post_call(kind, x_all, mod, att, y2, extras, rows, w_out, n_batch, ctx_len, tm, lat_only):
    bsz, t_all, _ = x_all.shape
    ctx_tiles = ctx_len // tm
    off = ctx_tiles if lat_only else 0
    nt = t_all // tm - off
    tok = lambda b, t: (b, t + off, 0)
    att_off = 0 if att.shape[1] == t_all else -off
    const = lambda b, t: (0, 0)
    in_specs = [pl.BlockSpec((1, tm, D_MODEL), tok),
                _mod_spec(n_batch, ctx_tiles, off),
                pl.BlockSpec((1, tm, MIX_W), lambda b, t: (b, t + off + att_off, 0)),
                pl.BlockSpec((1, 1, tm, MIX_W), lambda b, t: (0, b, t + off, 0)),
                pl.BlockSpec((1, 1, tm, MIX_W), lambda b, t: (1, b, t + off, 0))]
    in_specs += [pl.BlockSpec((1, tm, MIX_W), tok) for _ in extras]
    in_specs += [pl.BlockSpec(r.shape, const) for r in rows]
    in_specs += [pl.BlockSpec((D_MODEL, D_MODEL), const)]
    body = _post_rwkv_kernel if kind == "rwkv" else _post_ssd_kernel
    return pl.pallas_call(
        body,
        grid=(bsz, nt),
        in_specs=in_specs,
        out_specs=pl.BlockSpec((1, tm, D_MODEL), lambda b, t: (b, t, 0)),
        out_shape=jax.ShapeDtypeStruct((bsz, nt * tm, D_MODEL), F32),
        compiler_params=_cparams(("parallel", "arbitrary")),
        name="post_" + kind,
    )(x_all, mod, att, y2, y2, *extras, *rows, w_out)


def _mlp_kernel(x_ref, mod_ref, nw_ref, w1_ref, w2_ref, fw_ref, o_ref, *, final):
    x = x_ref[0]
    ms = jnp.mean(x * x, axis=-1, keepdims=True)
    h = x * lax.rsqrt(ms + NORM_EPS) * nw_ref[...]
    hb = (h * (1.0 + mod_ref[0, 4:5, :]) + mod_ref[0, 3:4, :]).astype(BF16)
    acc = jnp.zeros_like(x)
    for j in range(MLP_HIDDEN // D_MODEL):
        a = jnp.maximum(jnp.dot(hb, w1_ref[:, D_MODEL * j:D_MODEL * (j + 1)], preferred_element_type=F32), 0.0)
        acc = acc + jnp.dot((a * a).astype(BF16), w2_ref[D_MODEL * j:D_MODEL * (j + 1), :],
                            preferred_element_type=F32)
    x = x + mod_ref[0, 5:6, :] * acc
    if final:
        x = x * lax.rsqrt(jnp.mean(x * x, axis=-1, keepdims=True) + NORM_EPS) * fw_ref[...]
    o_ref[0] = x


def _mlp_call(x_in, mod, norm_w, w1, w2, final_w, n_batch, ctx_tiles, tm, final):
    bsz, t_in, _ = x_in.shape
    const = lambda b, t: (0, 0)
    tok = lambda b, t: (b, t, 0)
    weights = lambda shape: pl.BlockSpec(shape, const, pipeline_mode=pl.Buffered(1))
    return pl.pallas_call(
        functools.partial(_mlp_kernel, final=final),
        grid=(bsz, t_in // tm),
        in_specs=[pl.BlockSpec((1, tm, D_MODEL), tok),
                  _mod_spec(n_batch, ctx_tiles, 0),
                  pl.BlockSpec((1, D_MODEL), const),
                  weights((D_MODEL, MLP_HIDDEN)),
                  weights((MLP_HIDDEN, D_MODEL)),
                  pl.BlockSpec((1, D_MODEL), const)],
        out_specs=pl.BlockSpec((1, tm, D_MODEL), tok),
        out_shape=jax.ShapeDtypeStruct((bsz, t_in, D_MODEL), F32),
        compiler_params=_cparams(("parallel", "arbitrary")),
        name="mlp_final" if final else "mlp",
    )(x_in, mod, norm_w, w1, w2, final_w)


def _rope_tables(n_lat, ctx_len):
    rows = n_lat // GRID_W
    row = jnp.repeat(jnp.arange(rows, dtype=F32), GRID_W)
    col = jnp.tile(jnp.arange(GRID_W, dtype=F32), rows)
    n_freq = HEAD_DIM // 4
    inv_freq = ROPE_THETA ** (-jnp.arange(n_freq, dtype=F32) / n_freq)
    ang_r = row[:, None] * inv_freq
    ang_c = col[:, None] * inv_freq
    cr, sr, cc, sc = jnp.cos(ang_r), jnp.sin(ang_r), jnp.cos(ang_c), jnp.sin(ang_c)
    cos = jnp.concatenate([cr, cr, cc, cc], axis=-1)
    sin = jnp.concatenate([-sr, sr, -sc, sc], axis=-1)
    cos = jnp.concatenate([jnp.ones((ctx_len, HEAD_DIM), F32), cos], axis=0)
    sin = jnp.concatenate([jnp.zeros((ctx_len, HEAD_DIM), F32), sin], axis=0)
    return jnp.tile(cos, (1, N_HEADS)), jnp.tile(sin, (1, N_HEADS))


def _dup_kv_cols(w):
    k0, k1 = w[:, :HEAD_DIM], w[:, HEAD_DIM:]
    return jnp.concatenate([k0, k0, k1, k1], axis=1)


def _seg_matrix():
    return jnp.asarray(np.kron(np.eye(N_HEADS), np.ones((HEAD_DIM, HEAD_DIM))), BF16)


def _tile_rows(ctx_len):
    return 256 if ctx_len % 256 == 0 else 128


def _forward(x, c, ctx, c_ctx, w_mod, b_mod, norm_mix, norm_mlp, w_out, mlp_w1, mlp_w2, final_norm,
             ab_w_in, a_q_norm, a_k_norm, b_mu_prev, b_mu_next, b_w0, b_w2, b_a0, b_a2, b_g2,
             b_k_k, b_k_a, b_r_k, b_ln_w, b_ln_b,
             cd_w_in, c_sink, d_conv_w, d_conv_b, d_dt_bias, d_A_log, d_D, d_norm_w):
    bsz, n_lat, _ = x.shape
    ctx_len = ctx.shape[1]
    tm = _tile_rows(ctx_len)
    ctx_tiles = ctx_len // tm
    mod_rows = -(-(bsz + 1) // 8) * 8
    c_all = jnp.concatenate([c, c_ctx[None, :], jnp.zeros((mod_rows - bsz - 1, D_MODEL), F32)], axis=0)
    mods = _mod_call(c_all, w_mod, b_mod).reshape(DEPTH, mod_rows, 6, D_MODEL)
    cos, sin = _rope_tables(n_lat, ctx_len)
    seg = _seg_matrix()
    row = lambda a: a.reshape(1, -1).astype(F32)
    x_all = jnp.concatenate([ctx, x], axis=1)
    head_dup = lambda w, n: jnp.tile(w.astype(F32), n).reshape(1, -1)

    for i in range(DEPTH):
        j = i // 2
        last = i == DEPTH - 1
        mod = mods[i]
        if i % 2 == 0:
            w = ab_w_in[j]
            w_in = jnp.concatenate([w[:, :MIX_W], _dup_kv_cols(w[:, MIX_W:MIX_W + 128]),
                                    _dup_kv_cols(w[:, MIX_W + 128:MIX_W + 256]), w[:, MIX_W + 256:]],
                                   axis=1).astype(BF16)
            q, kdup, vdup, f = _premix_call(
                x_all, mod, row(norm_mix[i]), w_in, cos, sin, head_dup(a_q_norm[j], N_HEADS),
                head_dup(a_k_norm[j], 2 * KV_HEADS), seg, (MIX_W, KV_DUP_W, KV_DUP_W, B_IN),
                (BF16, BF16, BF16, F32), True, bsz, ctx_tiles, tm)
            att = _gqa_call(q, kdup, vdup, ctx_len)
            w2cat = jnp.zeros((B_LORA_W, 2 * MIX_W), F32)
            w2cat = w2cat.at[:64, :MIX_W].set(b_w2[j, 0]).at[64:, MIX_W:].set(b_w2[j, 1])
            a2cat = jnp.zeros((B_LORA_W, 2 * MIX_W), F32)
            a2cat = a2cat.at[:64, :MIX_W].set(b_a2[j, 0]).at[64:, MIX_W:].set(b_a2[j, 1])
            p = dict(mu_prev=row(b_mu_prev[j]), mu_next=row(b_mu_next[j]), w0=row(b_w0[j]),
                     w2=w2cat.astype(BF16), a0=row(b_a0[j]), a2=a2cat.astype(BF16),
                     g2=b_g2[j].astype(BF16), k_k=row(b_k_k[j]), k_a=row(b_k_a[j]), r_k=row(b_r_k[j]),
                     seg=seg)
            r, v, kk, lw, bb, kd, bonus, g = _rwkv_prep_call(f, p, ctx_len, 128)
            y2 = _rwkv_scan_call(r, v, kk, lw, bb, kd, ctx_len)
            x_all = _post_call("rwkv", x_all, mod, att, y2, [bonus, g],
                               [row(b_ln_w[j]), row(b_ln_b[j]), seg], w_out[i].astype(BF16),
                               bsz, ctx_len, tm, False)
        else:
            w = cd_w_in[j]
            o = MIX_W + 256
            w_in = jnp.concatenate([w[:, :MIX_W], _dup_kv_cols(w[:, MIX_W:MIX_W + 128]),
                                    _dup_kv_cols(w[:, MIX_W + 128:o]), w[:, o:],
                                    jnp.zeros((D_MODEL, DT_PAD - 2 * N_HEADS), F32)], axis=1).astype(BF16)
            q, kdup, vdup, z, xbc, dt_raw = _premix_call(
                x_all, mod, row(norm_mix[i]), w_in, cos, sin, jnp.ones((1, MIX_W), F32),
                jnp.ones((1, KV_DUP_W), F32), seg, (MIX_W, KV_DUP_W, KV_DUP_W, MIX_W, D_XBC, DT_PAD),
                (BF16, BF16, BF16, F32, F32, F32), False, bsz, ctx_tiles, tm)
            att = _swa_call(c_sink[j].astype(F32), q, kdup, vdup, ctx_len, last)
            u = _conv_call(xbc, d_conv_w[j], row(d_conv_b[j]), ctx_len, 128)
            pad = jnp.zeros((DT_PAD - 2 * N_HEADS,), F32)
            dtb = jnp.concatenate([d_dt_bias[j].reshape(-1), pad]).reshape(1, -1)
            a_row = jnp.concatenate([-jnp.exp(d_A_log[j].astype(F32)).reshape(-1), pad]).reshape(1, -1)
            sel = np.zeros((2, DT_PAD, DT_PAD), np.float32)
            expand = np.zeros((DT_PAD, MIX_W), np.float32)
            for h in range(N_HEADS):
                sel[0, h, h] = 1.0
                sel[1, N_HEADS + h, h] = 1.0
                expand[h, HEAD_DIM * h:HEAD_DIM * (h + 1)] = 1.0
            y2 = _ssd_scan_call(u, dt_raw, dtb, a_row, jnp.asarray(sel, BF16), jnp.asarray(expand, BF16), ctx_len)
            dsk = jnp.repeat(d_D[j].astype(F32), HEAD_DIM).reshape(1, -1)
            x_all = _post_call("ssd", x_all, mod, att, y2, [u, z], [dsk, row(d_norm_w[j])],
                               w_out[i].astype(BF16), bsz, ctx_len, tm, last)
        x_all = _mlp_call(x_all, mod, row(norm_mlp[i]), mlp_w1[i].astype(BF16), mlp_w2[i].astype(BF16),
                          row(final_norm), bsz, 0 if last else ctx_tiles, tm, last)
    return x_all


def kernel(x, c, ctx, c_ctx, w_mod, b_mod, norm_mix, norm_mlp, w_out, mlp_w1, mlp_w2, final_norm, ab_w_in, a_q_norm, a_k_norm, b_mu_prev, b_mu_next, b_w0, b_w2, b_a0, b_a2, b_g2, b_k_k, b_k_a, b_r_k, b_ln_w, b_ln_b, cd_w_in, c_sink, d_conv_w, d_conv_b, d_dt_bias, d_A_log, d_D, d_norm_w):
    return _forward(x, c, ctx, c_ctx, w_mod, b_mod, norm_mix, norm_mlp, w_out, mlp_w1, mlp_w2, final_norm,
                    ab_w_in, a_q_norm, a_k_norm, b_mu_prev, b_mu_next, b_w0, b_w2, b_a0, b_a2, b_g2,
                    b_k_k, b_k_a, b_r_k, b_ln_w, b_ln_b,
                    cd_w_in, c_sink, d_conv_w, d_conv_b, d_dt_bias, d_A_log, d_D, d_norm_w)
```

```python
import functools
import math

import numpy as np
import jax
import jax.numpy as jnp
from jax import lax
from jax.experimental import pallas as pl
from jax.experimental.pallas import tpu as pltpu

F32 = jnp.float32
BF16 = jnp.bfloat16

D_MODEL = 1024
DEPTH = 4
GRID_W = 64
HEAD_DIM = 64
ROPE_THETA = 10000.0
NORM_EPS = 1e-6
MLP_HIDDEN = 4 * D_MODEL
N_HEADS = 8
MIX_W = N_HEADS * HEAD_DIM
KV_HEADS = 2
KV_W = KV_HEADS * HEAD_DIM
KV_DUP_W = 2 * KV_W
B_LORA_W = 128
B_IN = 3 * MIX_W + 3 * B_LORA_W
B_GN_EPS = 64e-5
WINDOW = 128
D_STATE = 128
D_GROUPS = 2
D_CONV = 5
D_XBC = MIX_W + 2 * D_GROUPS * D_STATE
DT_PAD = 128
SSD_CHUNK = 128
RWKV_CHUNK = 64
ATT_TQ = 128
LOG2E = math.log2(math.e)
Q_SCALE = HEAD_DIM ** -0.5 * LOG2E
ONES_ROWS = 16
HALO = 8
VMEM_LIMIT = 56 * 1024 * 1024


def _cparams(sem):
    return pltpu.CompilerParams(dimension_semantics=sem, vmem_limit_bytes=VMEM_LIMIT)


def _dot(a, b):
    return jnp.dot(a.astype(BF16), b.astype(BF16), preferred_element_type=F32)


def _dot_nt(a, b):
    return lax.dot_general(a.astype(BF16), b.astype(BF16), (((1,), (1,)), ((), ())),
                           preferred_element_type=F32)


def _split_hi_lo(x):
    hi = x.astype(BF16)
    lo = (x - hi.astype(F32)).astype(BF16)
    return hi, lo


def _dot_exact_rhs(x, m):
    hi, lo = _split_hi_lo(x)
    return (jnp.dot(hi, m, preferred_element_type=F32)
            + jnp.dot(lo, m, preferred_element_type=F32))


def _dot_exact_lhs(m, x):
    hi, lo = _split_hi_lo(x)
    return (jnp.dot(m, hi, preferred_element_type=F32)
            + jnp.dot(m, lo, preferred_element_type=F32))


def _rope(x, cos, sin_signed):
    w = x.shape[-1]
    lane = lax.broadcasted_iota(jnp.int32, x.shape, 1)
    partner = jnp.where((lane & 16) == 0, pltpu.roll(x, w - 16, 1), pltpu.roll(x, 16, 1))
    return x * cos + partner * sin_signed


def _shift_rows(cur, prev8, next8, s):
    n = cur.shape[0]
    if s == 0:
        return cur
    ri = lax.broadcasted_iota(jnp.int32, (HALO, cur.shape[1]), 0)
    if s > 0:
        rolled = pltpu.roll(cur, s, 0)
        fill = pltpu.roll(prev8, s, 0)
        head = jnp.where(ri < s, fill, rolled[:HALO])
        return jnp.concatenate([head, rolled[HALO:]], axis=0)
    a = -s
    rolled = pltpu.roll(cur, n - a, 0)
    fill = pltpu.roll(next8, HALO - a, 0)
    tail = jnp.where(ri >= HALO - a, fill, rolled[n - HALO:])
    return jnp.concatenate([rolled[:n - HALO], tail], axis=0)


def _sigmoid(x):
    return 1.0 / (1.0 + jnp.exp(-x))


def _softplus(x):
    return jnp.maximum(x, 0.0) + jnp.log(1.0 + jnp.exp(-jnp.abs(x)))


def _mod_kernel(c_ref, w_ref, b_ref, o_ref):
    c = c_ref[...]
    o_ref[0] = _dot(c * _sigmoid(c), w_ref[0]) + b_ref[0]


def _mod_call(c_all, w_mod, b_mod):
    rows = c_all.shape[0]
    n_col = w_mod.shape[2] // D_MODEL
    return pl.pallas_call(
        _mod_kernel,
        grid=(DEPTH, n_col),
        in_specs=[pl.BlockSpec((rows, D_MODEL), lambda i, j: (0, 0)),
                  pl.BlockSpec((1, D_MODEL, D_MODEL), lambda i, j: (i, 0, j)),
                  pl.BlockSpec((1, 1, D_MODEL), lambda i, j: (i, 0, j))],
        out_specs=pl.BlockSpec((1, rows, D_MODEL), lambda i, j: (i, 0, j)),
        out_shape=jax.ShapeDtypeStruct((DEPTH, rows, w_mod.shape[2]), F32),
        compiler_params=_cparams(("arbitrary", "arbitrary")),
        name="modulation",
    )(c_all, w_mod, b_mod.reshape(DEPTH, 1, -1))


def _mod_spec(n_batch, ctx_tiles, tile_off):
    return pl.BlockSpec((1, 6, D_MODEL),
                        lambda b, t: (jnp.where(t + tile_off < ctx_tiles, n_batch, b), 0, 0))


def _head_rms(y, seg):
    w = y.shape[-1]
    ssum = _dot(y * y, seg[:w, :w])
    return y * lax.rsqrt(ssum * (1.0 / HEAD_DIM) + NORM_EPS)


def _premix_kernel(x_ref, mod_ref, nw_ref, w_ref, cos_ref, sin_ref, qnw_ref, knw_ref, seg_ref,
                   *out_refs, widths, qk_norm):
    x = x_ref[0]
    ms = jnp.mean(x * x, axis=-1, keepdims=True)
    h = x * lax.rsqrt(ms + NORM_EPS) * nw_ref[...]
    h = h * (1.0 + mod_ref[0, 1:2, :]) + mod_ref[0, 0:1, :]
    hb = h.astype(BF16)
    off = 0
    for idx, (o_ref, wdt) in enumerate(zip(out_refs, widths)):
        y = jnp.dot(hb, w_ref[:, off:off + wdt], preferred_element_type=F32)
        if idx < 2:
            if qk_norm:
                nw = qnw_ref if idx == 0 else knw_ref
                y = _head_rms(y, seg_ref[...]) * nw[...]
            y = _rope(y, cos_ref[:, :wdt], sin_ref[:, :wdt])
            if idx == 0:
                y = y * Q_SCALE
        if idx == 2:
            o_ref[0, 0] = y.T.astype(o_ref.dtype)
        else:
            o_ref[0] = y.astype(o_ref.dtype)
        off += wdt


def _premix_call(x_all, mod, norm_w, w_in, cos, sin, qnw, knw, seg, widths, dtypes, qk_norm,
                 n_batch, ctx_tiles, tm):
    bsz, t_all, _ = x_all.shape
    nt = t_all // tm
    n_in = w_in.shape[1]
    const = lambda b, t: (0, 0)
    tok = lambda b, t: (b, t, 0)
    out_specs = [pl.BlockSpec((1, tm, w), tok) for w in widths]
    out_shape = [jax.ShapeDtypeStruct((bsz, t_all, w), d) for w, d in zip(widths, dtypes)]
    out_specs[2] = pl.BlockSpec((1, 1, widths[2], tm), lambda b, t: (b, t, 0, 0))
    out_shape[2] = jax.ShapeDtypeStruct((bsz, nt, widths[2], tm), dtypes[2])
    return pl.pallas_call(
        functools.partial(_premix_kernel, widths=widths, qk_norm=qk_norm),
        grid=(bsz, nt),
        in_specs=[pl.BlockSpec((1, tm, D_MODEL), tok),
                  _mod_spec(n_batch, ctx_tiles, 0),
                  pl.BlockSpec((1, D_MODEL), const),
                  pl.BlockSpec((D_MODEL, n_in), const),
                  pl.BlockSpec((tm, MIX_W), lambda b, t: (t, 0)),
                  pl.BlockSpec((tm, MIX_W), lambda b, t: (t, 0)),
                  pl.BlockSpec((1, MIX_W), const),
                  pl.BlockSpec((1, KV_DUP_W), const),
                  pl.BlockSpec((MIX_W, MIX_W), const)],
        out_specs=out_specs,
        out_shape=out_shape,
        compiler_params=_cparams(("parallel", "arbitrary")),
        name="premix_qknorm" if qk_norm else "premix",
    )(x_all, mod, norm_w, w_in, cos, sin, qnw, knw, seg)


NEG_BIG = -1e30


def _masked_q_heads(q_ref):
    tq = q_ref.shape[1]
    lane = lax.broadcasted_iota(jnp.int32, (tq, 2 * HEAD_DIM), 1)
    heads = []
    for h in range(N_HEADS):
        slab = q_ref[0, :, 128 * (h // 2):128 * (h // 2) + 128]
        keep = (lane < HEAD_DIM) if h % 2 == 0 else (lane >= HEAD_DIM)
        heads.append(jnp.where(keep, slab, jnp.zeros_like(slab)))
    return heads


def _fold_rows_max(s):
    m = s[0:8]
    for i in range(1, s.shape[0] // 8):
        m = jnp.maximum(m, s[8 * i:8 * i + 8])
    return m


def _attend(q_ref, o_ref, blocks, sinks):
    tq = q_ref.shape[1]
    q_heads = _masked_q_heads(q_ref)

    def scores(kc, h, mask):
        s = _dot_nt(kc, q_heads[h])
        return s if mask is None else jnp.where(mask, s, NEG_BIG)

    mx = [jnp.full((8, tq), NEG_BIG, F32) for _ in range(N_HEADS)]
    for k_of, _, mask in blocks:
        for g in range(KV_HEADS):
            kc = k_of(g)
            for r in range(N_HEADS // KV_HEADS):
                h = g * (N_HEADS // KV_HEADS) + r
                mx[h] = jnp.maximum(mx[h], _fold_rows_max(scores(kc, h, mask)))
    m = [jnp.max(x, axis=0, keepdims=True) for x in mx]
    if sinks is not None:
        m = [jnp.maximum(mh, sk) for mh, sk in zip(m, sinks)]

    acc = [jnp.zeros((HEAD_DIM + ONES_ROWS, tq), F32) for _ in range(N_HEADS)]
    for k_of, vt_of, mask in blocks:
        for g in range(KV_HEADS):
            kc = k_of(g)
            vt = vt_of(g)
            vt = jnp.concatenate([vt, jnp.ones((ONES_ROWS, vt.shape[1]), BF16)], axis=0)
            for r in range(N_HEADS // KV_HEADS):
                h = g * (N_HEADS // KV_HEADS) + r
                p = jnp.exp2(scores(kc, h, mask) - m[h]).astype(BF16)
                acc[h] = acc[h] + jnp.dot(vt, p, preferred_element_type=F32)
    outs = []
    for h in range(N_HEADS):
        l = acc[h][HEAD_DIM:HEAD_DIM + 1]
        if sinks is not None:
            l = l + jnp.exp2(sinks[h] - m[h])
        outs.append(acc[h][:HEAD_DIM] / l)
    o_ref[0] = jnp.concatenate(outs, axis=0).T.astype(o_ref.dtype)


def _gqa_kernel(q_ref, k_ref, vt_ref, o_ref, *, ctx_chunks, ctx_tiles):
    kc_rows = vt_ref.shape[3]

    def blocks(n_chunks):
        return [(lambda g, j=j: k_ref[0, j * kc_rows:(j + 1) * kc_rows, 128 * g:128 * g + 128],
                 lambda g, j=j: vt_ref[0, j, HEAD_DIM * g:HEAD_DIM * (g + 1), :],
                 None) for j in range(n_chunks)]

    t = pl.program_id(1)

    @pl.when(t < ctx_tiles)
    def _():
        _attend(q_ref, o_ref, blocks(ctx_chunks), None)

    @pl.when(t >= ctx_tiles)
    def _():
        _attend(q_ref, o_ref, blocks(vt_ref.shape[1]), None)


def _gqa_call(q, kdup, vt, ctx_len):
    bsz, t_all, _ = q.shape
    _, n_chunks, vw, kc_rows = vt.shape
    tq = ATT_TQ
    return pl.pallas_call(
        functools.partial(_gqa_kernel, ctx_chunks=ctx_len // kc_rows, ctx_tiles=ctx_len // tq),
        grid=(bsz, t_all // tq),
        in_specs=[pl.BlockSpec((1, tq, MIX_W), lambda b, t: (b, t, 0)),
                  pl.BlockSpec((1, t_all, KV_DUP_W), lambda b, t: (b, 0, 0)),
                  pl.BlockSpec((1, n_chunks, vw, kc_rows), lambda b, t: (b, 0, 0, 0))],
        out_specs=pl.BlockSpec((1, tq, MIX_W), lambda b, t: (b, t, 0)),
        out_shape=jax.ShapeDtypeStruct((bsz, t_all, MIX_W), BF16),
        compiler_params=_cparams(("parallel", "arbitrary")),
        name="global_gqa",
    )(q, kdup, vt)


def _swa_kernel(sink_ref, q_ref, kc_ref, vtc_ref, kp_ref, kq_ref, kn_ref, vtp_ref, vtq_ref, vtn_ref,
                o_ref, *, ctx_tiles, tile_off, n_tiles_all):
    tq = q_ref.shape[1]
    t = pl.program_id(1) + tile_off
    sinks = [jnp.full((1, tq), sink_ref[h] * LOG2E, F32) for h in range(N_HEADS)]
    kc_rows = vtc_ref.shape[3]
    ctx_blocks = [(lambda g, j=j: kc_ref[0, j * kc_rows:(j + 1) * kc_rows, 128 * g:128 * g + 128],
                   lambda g, j=j: vtc_ref[0, j, HEAD_DIM * g:HEAD_DIM * (g + 1), :],
                   None) for j in range(vtc_ref.shape[1])]

    def near(k_ref, vt_ref, mask):
        return (lambda g: k_ref[0, :, 128 * g:128 * g + 128],
                lambda g: vt_ref[0, 0, HEAD_DIM * g:HEAD_DIM * (g + 1), :], mask)

    @pl.when(t < ctx_tiles)
    def _():
        _attend(q_ref, o_ref, ctx_blocks, sinks)

    @pl.when(t >= ctx_tiles)
    def _():
        key_i = lax.broadcasted_iota(jnp.int32, (tq, tq), 0)
        qry_i = lax.broadcasted_iota(jnp.int32, (tq, tq), 1)
        lim_prev = jnp.where(t > ctx_tiles, 0, tq)
        lim_next = jnp.where(t < n_tiles_all - 1, 0, tq)
        _attend(q_ref, o_ref,
                ctx_blocks + [near(kp_ref, vtp_ref, key_i - qry_i >= lim_prev),
                              near(kq_ref, vtq_ref, None),
                              near(kn_ref, vtn_ref, qry_i - key_i >= lim_next)], sinks)


def _swa_call(sink, q, kdup, vt, ctx_len, lat_only):
    bsz, t_all, _ = q.shape
    _, _, vw, kc_rows = vt.shape
    tq = ATT_TQ
    per = kc_rows // tq
    ctx_tiles = ctx_len // tq
    n_all = t_all // tq
    off = ctx_tiles if lat_only else 0
    cur = lambda b, t: t + off
    prv = lambda b, t: jnp.maximum(t + off - 1, ctx_tiles)
    nxt = lambda b, t: jnp.minimum(t + off + 1, n_all - 1)
    k_spec = lambda tile: pl.BlockSpec((1, tq, KV_DUP_W), lambda b, t: (b, tile(b, t), 0))
    vt_spec = lambda tile: pl.BlockSpec((1, 1, vw, tq),
                                        lambda b, t: (b, tile(b, t) // per, 0, tile(b, t) % per))
    return pl.pallas_call(
        functools.partial(_swa_kernel, ctx_tiles=ctx_tiles, tile_off=off, n_tiles_all=n_all),
        grid=(bsz, n_all - off),
        in_specs=[pl.BlockSpec(memory_space=pltpu.SMEM),
                  pl.BlockSpec((1, tq, MIX_W), lambda b, t: (b, t + off, 0)),
                  pl.BlockSpec((1, ctx_len, KV_DUP_W), lambda b, t: (b, 0, 0)),
                  pl.BlockSpec((1, ctx_len // kc_rows, vw, kc_rows), lambda b, t: (b, 0, 0, 0)),
                  k_spec(prv), k_spec(cur), k_spec(nxt), vt_spec(prv), vt_spec(cur), vt_spec(nxt)],
        out_specs=pl.BlockSpec((1, tq, MIX_W), lambda b, t: (b, t, 0)),
        out_shape=jax.ShapeDtypeStruct((bsz, t_all - off * tq, MIX_W), BF16),
        compiler_params=_cparams(("parallel", "arbitrary")),
        name="window_gqa",
    )(sink, q, kdup, vt, kdup, kdup, kdup, vt, vt, vt)


def _segment_flags(t, ctx_tiles, n_tiles):
    first = (t == 0) | (t == ctx_tiles)
    last = (t == ctx_tiles - 1) | (t == n_tiles - 1)
    return first, last


def _rwkv_prep_kernel(f_ref, fp_ref, fn_ref, mup_ref, mun_ref, w0_ref, w2_ref, a0_ref, a2_ref, g2_ref,
                      kk_w_ref, ka_ref, rk_ref, seg_ref,
                      r_ref, v_ref, kk_ref, lw_ref, bb_ref, kd_ref, bonus_ref, g_ref,
                      *, ctx_tiles, n_tiles):
    first, last = _segment_flags(pl.program_id(1), ctx_tiles, n_tiles)
    cur = f_ref[0]
    prev8 = jnp.where(first, 0.0, fp_ref[0])
    next8 = jnp.where(last, 0.0, fn_ref[0])
    prv = _shift_rows(cur, prev8, next8, 1)
    nxt = _shift_rows(cur, prev8, next8, -1)
    f = cur + mup_ref[...] * (prv - cur) + mun_ref[...] * (nxt - cur)
    r = f[:, 0:MIX_W]
    k = f[:, MIX_W:2 * MIX_W]
    v = f[:, 2 * MIX_W:3 * MIX_W]
    wd = f[:, 3 * MIX_W:3 * MIX_W + B_LORA_W]
    ad = f[:, 3 * MIX_W + B_LORA_W:3 * MIX_W + 2 * B_LORA_W]
    gd = f[:, 3 * MIX_W + 2 * B_LORA_W:]
    wlog = w0_ref[...] + _dot(jnp.tanh(wd), w2_ref[...])
    lw_ref[0] = -math.exp(-0.5) * _sigmoid(wlog)
    a = _sigmoid(a0_ref[...] + _dot(ad, a2_ref[...]))
    kk = k * kk_w_ref[...]
    kk = kk * lax.rsqrt(_dot_exact_rhs(kk * kk, seg_ref[...]) + 1e-12)
    ksum = jnp.zeros_like(k)
    for d in range(2):
        a_d = a[:, d * MIX_W:(d + 1) * MIX_W]
        k_d = k * (1.0 + (a_d - 1.0) * ka_ref[...])
        bb_ref[0, :, d * MIX_W:(d + 1) * MIX_W] = a_d * kk
        kd_ref[0, :, d * MIX_W:(d + 1) * MIX_W] = k_d
        ksum = ksum + k_d
    r_ref[0] = r
    v_ref[0] = v
    kk_ref[0] = kk
    bonus_ref[0] = _dot_exact_rhs(r * ksum * rk_ref[...], seg_ref[...]) * v
    g_ref[0] = _dot(_sigmoid(gd), g2_ref[...])


def _rwkv_prep_call(f, p, ctx_len, tm):
    bsz, t_all, _ = f.shape
    nt = t_all // tm
    ctx_tiles = ctx_len // tm
    nb8 = t_all // HALO
    const = lambda b, t: (0, 0)
    tok = lambda b, t: (b, t, 0)
    wide = jax.ShapeDtypeStruct((bsz, t_all, 2 * MIX_W), F32)
    one = jax.ShapeDtypeStruct((bsz, t_all, MIX_W), F32)
    row = lambda w: pl.BlockSpec((1, w), const)
    return pl.pallas_call(
        functools.partial(_rwkv_prep_kernel, ctx_tiles=ctx_tiles, n_tiles=nt),
        grid=(bsz, nt),
        in_specs=[pl.BlockSpec((1, tm, B_IN), tok),
                  pl.BlockSpec((1, HALO, B_IN), lambda b, t: (b, jnp.maximum(t * (tm // HALO) - 1, 0), 0)),
                  pl.BlockSpec((1, HALO, B_IN), lambda b, t: (b, jnp.minimum((t + 1) * (tm // HALO), nb8 - 1), 0)),
                  row(B_IN), row(B_IN),
                  row(2 * MIX_W), pl.BlockSpec((B_LORA_W, 2 * MIX_W), const),
                  row(2 * MIX_W), pl.BlockSpec((B_LORA_W, 2 * MIX_W), const),
                  pl.BlockSpec((B_LORA_W, MIX_W), const),
                  row(MIX_W), row(MIX_W), row(MIX_W),
                  pl.BlockSpec((MIX_W, MIX_W), const)],
        out_specs=[pl.BlockSpec((1, tm, MIX_W), tok)] * 3 + [pl.BlockSpec((1, tm, 2 * MIX_W), tok)] * 3
                  + [pl.BlockSpec((1, tm, MIX_W), tok)] * 2,
        out_shape=[one, one, one, wide, wide, wide, one, one],
        compiler_params=_cparams(("parallel", "arbitrary")),
        name="rwkv_prep",
    )(f, f, f, p["mu_prev"], p["mu_next"], p["w0"], p["w2"], p["a0"], p["a2"], p["g2"],
      p["k_k"], p["k_a"], p["r_k"], p["seg"])


def _scan_chunk(s, d, n_ctx, n_all):
    bwd = jnp.where(s < n_ctx, n_ctx - 1 - s, n_all - 1 + n_ctx - s)
    return jnp.where(d == 0, s, bwd)


def _rwkv_scan_kernel(rf_ref, vf_ref, kkf_ref, lwf_ref, bbf_ref, kdf_ref,
                      rb_ref, vb_ref, kkb_ref, lwb_ref, bbb_ref, kdb_ref,
                      yf_ref, yb_ref, state_ref):
    c = rf_ref.shape[1]

    @pl.when(pl.program_id(1) == 0)
    def _():
        state_ref[...] = jnp.zeros_like(state_ref)

    ri = lax.broadcasted_iota(jnp.int32, (c, c), 0)
    ci = lax.broadcasted_iota(jnp.int32, (c, c), 1)
    eye = jnp.where(ri == ci, 1.0, 0.0)

    units = []
    dir_refs = ((rf_ref, vf_ref, kkf_ref, lwf_ref, bbf_ref, kdf_ref),
                (rb_ref, vb_ref, kkb_ref, lwb_ref, bbb_ref, kdb_ref))
    for d, (r_ref, v_ref, kk_ref, lw_ref, bb_ref, kd_ref) in enumerate(dir_refs):
        ahead = ri - ci if d == 0 else ci - ri
        incl = ahead >= 0
        strict = ahead > 0
        lw = lw_ref[0]
        cum = _dot_exact_lhs(jnp.where(incl, 1.0, 0.0).astype(BF16), lw)
        e_cum = jnp.exp(cum)
        e_inv = jnp.exp(-cum)
        e_tot = jnp.exp(jnp.sum(lw, axis=0, keepdims=True))
        qt = r_ref[0] * e_cum
        kkt = kk_ref[0] * jnp.exp(cum - lw)
        kh = kd_ref[0] * e_inv
        bh = bb_ref[0] * e_inv
        k_end = kh * e_tot
        b_end = -(bh * e_tot)
        v = v_ref[0]
        for h in range(N_HEADS):
            sl = slice(HEAD_DIM * h, HEAD_DIM * (h + 1))
            units.append(dict(
                d=d, h=h, incl=incl, strict=strict, v=v[:, sl], e_tot=e_tot[:, sl],
                lhs=jnp.concatenate([qt[:, sl], kkt[:, sl]], axis=0).astype(BF16),
                rhs=jnp.concatenate([kh[:, sl], bh[:, sl]], axis=0).astype(BF16),
                x_end=jnp.concatenate([k_end[:, sl], b_end[:, sl]], axis=0).astype(BF16)))

    for u in units:
        a = _dot_nt(u["lhs"], u["rhs"])
        u["a_v"] = jnp.concatenate([jnp.where(u["incl"], a[:c, :c], 0.0),
                                    jnp.where(u["strict"], a[c:, :c], 0.0)], axis=0).astype(BF16)
        u["a_rb"] = jnp.where(u["incl"], a[:c, c:], 0.0).astype(BF16)
        u["p"] = -jnp.where(u["strict"], a[c:, c:], 0.0)
    for u in units:
        u["t"] = eye + u["p"]
        u["p"] = _dot(u["p"], u["p"])
    m = 2
    while 2 * m < c:
        for u in units:
            both = _dot(jnp.concatenate([u["t"], u["p"]], axis=0), u["p"])
            u["t"] = u["t"] + both[:c]
            u["p"] = both[c:]
        m *= 2
    for u in units:
        u["t"] = (u["t"] + _dot(u["t"], u["p"])).astype(BF16)
        u["av"] = _dot(u["a_v"], u["v"])
    for u in units:
        u["st"] = state_ref[u["d"], u["h"]]
        u["x1"] = _dot_nt(u["lhs"], u["st"]) + u["av"]
    for u in units:
        u["u"] = _dot(u["t"], u["x1"][c:])
    ys = ([], [])
    for u in units:
        ys[u["d"]].append(u["x1"][:c] - _dot(u["a_rb"], u["u"]))
        wv = jnp.concatenate([u["v"], u["u"]], axis=0)
        state_ref[u["d"], u["h"]] = u["st"] * u["e_tot"] + _dot(wv.T, u["x_end"])
    yf_ref[0] = jnp.concatenate(ys[0], axis=1)
    yb_ref[0] = jnp.concatenate(ys[1], axis=1)


def _rwkv_scan_call(r, v, kk, lw, bb, kd, ctx_len):
    bsz, t_all, _ = r.shape
    c = RWKV_CHUNK
    n_all = t_all // c
    n_ctx = ctx_len // c
    blk = lambda im: pl.BlockSpec((1, c, MIX_W), im)
    fwd = lambda b, s: (b, s, 0)
    bwd = lambda b, s: (b, _scan_chunk(s, 1, n_ctx, n_all), 0)
    bwd1 = lambda b, s: (b, _scan_chunk(s, 1, n_ctx, n_all), 1)
    out = jax.ShapeDtypeStruct((bsz, t_all, MIX_W), F32)
    return pl.pallas_call(
        _rwkv_scan_kernel,
        grid=(bsz, n_all),
        in_specs=[blk(fwd)] * 6 + [blk(bwd)] * 3 + [blk(bwd1)] * 3,
        out_specs=[blk(fwd), blk(bwd)],
        out_shape=[out, out],
        scratch_shapes=[pltpu.VMEM((2, N_HEADS, HEAD_DIM, HEAD_DIM), F32)],
        compiler_params=_cparams(("parallel", "arbitrary")),
        name="rwkv_scan",
    )(r, v, kk, lw, bb, kd, r, v, kk, lw, bb, kd)


def _conv_kernel(x_ref, xp_ref, xn_ref, w_ref, b_ref, o_ref, *, ctx_tiles, n_tiles):
    first, last = _segment_flags(pl.program_id(1), ctx_tiles, n_tiles)
    cur = x_ref[0]
    prev8 = jnp.where(first, 0.0, xp_ref[0])
    next8 = jnp.where(last, 0.0, xn_ref[0])
    acc = b_ref[...] + w_ref[D_CONV // 2:D_CONV // 2 + 1, :] * cur
    for kpos in range(D_CONV):
        s = D_CONV // 2 - kpos
        if s != 0:
            acc = acc + w_ref[kpos:kpos + 1, :] * _shift_rows(cur, prev8, next8, s)
    o_ref[0] = acc * _sigmoid(acc)


def _conv_call(xbc, conv_w, conv_b, ctx_len, tm):
    bsz, t_all, w = xbc.shape
    nt = t_all // tm
    nb8 = t_all // HALO
    return pl.pallas_call(
        functools.partial(_conv_kernel, ctx_tiles=ctx_len // tm, n_tiles=nt),
        grid=(bsz, nt),
        in_specs=[pl.BlockSpec((1, tm, w), lambda b, t: (b, t, 0)),
                  pl.BlockSpec((1, HALO, w), lambda b, t: (b, jnp.maximum(t * (tm // HALO) - 1, 0), 0)),
                  pl.BlockSpec((1, HALO, w), lambda b, t: (b, jnp.minimum((t + 1) * (tm // HALO), nb8 - 1), 0)),
                  pl.BlockSpec((D_CONV, w), lambda b, t: (0, 0)),
                  pl.BlockSpec((1, w), lambda b, t: (0, 0))],
        out_specs=pl.BlockSpec((1, tm, w), lambda b, t: (b, t, 0)),
        out_shape=jax.ShapeDtypeStruct((bsz, t_all, w), F32),
        compiler_params=_cparams(("parallel", "arbitrary")),
        name="ssd_conv",
    )(xbc, xbc, xbc, conv_w, conv_b)


def _ssd_scan_kernel(uf_ref, dtf_ref, ub_ref, dtb_ref, bias_ref, a_ref, sel_ref, exp_ref,
                     yf_ref, yb_ref, state_ref):
    c = uf_ref.shape[1]

    @pl.when(pl.program_id(1) == 0)
    def _():
        state_ref[...] = jnp.zeros_like(state_ref)

    ri = lax.broadcasted_iota(jnp.int32, (c, c), 0)
    ci = lax.broadcasted_iota(jnp.int32, (c, c), 1)
    expand = exp_ref[...]
    gw = D_STATE
    gl = MIX_W // D_GROUPS
    hpg = N_HEADS // D_GROUPS
    dirs = []
    for d, (u_ref, dt_ref) in enumerate(((uf_ref, dtf_ref), (ub_ref, dtb_ref))):
        incl = (ri - ci if d == 0 else ci - ri) >= 0
        incl_m = jnp.where(incl, 1.0, 0.0).astype(BF16)
        dt_all = _softplus(dt_ref[0] + bias_ref[...])
        dt_d = _dot_exact_rhs(dt_all, sel_ref[d])
        dta_d = _dot_exact_rhs(dt_all * a_ref[...], sel_ref[d])
        cum_d = _dot_exact_lhs(incl_m, dta_d)
        dta_x = _dot_exact_rhs(dta_d, expand)
        dirs.append(dict(u_ref=u_ref, incl=incl, cum_d=cum_d, cum_t=cum_d.T, dt_t=dt_d.T,
                         cum_x=_dot_exact_lhs(incl_m, dta_x), dt_x=_dot_exact_rhs(dt_d, expand),
                         tot_x=jnp.sum(dta_x, axis=0, keepdims=True), xs=u_ref[0, :, 0:MIX_W], ys=[]))

    def b_mat(u_ref, g):
        return u_ref[0, :, MIX_W + gw * g:MIX_W + gw * (g + 1)]

    def c_mat(u_ref, g):
        return u_ref[0, :, MIX_W + gw * D_GROUPS + gw * g:MIX_W + gw * D_GROUPS + gw * (g + 1)]

    for g in range(D_GROUPS):
        for p in dirs:
            p["cb"] = _dot_nt(c_mat(p["u_ref"], g), b_mat(p["u_ref"], g))
        for r in range(hpg):
            h = g * hpg + r
            for p in dirs:
                seg = p["cum_d"][:, h:h + 1] - p["cum_t"][h:h + 1, :]
                lmat = jnp.exp(jnp.where(p["incl"], seg, -jnp.inf))
                mm = p["cb"] * lmat * p["dt_t"][h:h + 1, :]
                p["ys"].append(_dot(mm, p["xs"][:, HEAD_DIM * h:HEAD_DIM * (h + 1)]))

    for d, (p, y_ref) in enumerate(zip(dirs, (yf_ref, yb_ref))):
        wj = jnp.exp(p["tot_x"] - p["cum_x"]) * p["dt_x"]
        xw = p["xs"] * wj
        e_tot = jnp.exp(p["tot_x"])
        offs = []
        for g in range(D_GROUPS):
            st = state_ref[d, :, gl * g:gl * (g + 1)]
            offs.append(_dot(c_mat(p["u_ref"], g), st))
            state_ref[d, :, gl * g:gl * (g + 1)] = (st * e_tot[:, gl * g:gl * (g + 1)]
                                                    + _dot(b_mat(p["u_ref"], g).T, xw[:, gl * g:gl * (g + 1)]))
        y_ref[0] = (jnp.concatenate(p["ys"], axis=1)
                    + jnp.concatenate(offs, axis=1) * jnp.exp(p["cum_x"]))


def _ssd_scan_call(u, dt_raw, dtb, a_row, sel, expand, ctx_len):
    bsz, t_all, w = u.shape
    c = SSD_CHUNK
    n_all = t_all // c
    n_ctx = ctx_len // c
    fwd = lambda b, s: (b, s, 0)
    bwd = lambda b, s: (b, _scan_chunk(s, 1, n_ctx, n_all), 0)
    const = lambda b, s: (0, 0)
    out = jax.ShapeDtypeStruct((bsz, t_all, MIX_W), F32)
    return pl.pallas_call(
        _ssd_scan_kernel,
        grid=(bsz, n_all),
        in_specs=[pl.BlockSpec((1, c, w), fwd), pl.BlockSpec((1, c, DT_PAD), fwd),
                  pl.BlockSpec((1, c, w), bwd), pl.BlockSpec((1, c, DT_PAD), bwd),
                  pl.BlockSpec((1, DT_PAD), const),
                  pl.BlockSpec((1, DT_PAD), const),
                  pl.BlockSpec((2, DT_PAD, DT_PAD), lambda b, s: (0, 0, 0)),
                  pl.BlockSpec((DT_PAD, MIX_W), const)],
        out_specs=[pl.BlockSpec((1, c, MIX_W), fwd), pl.BlockSpec((1, c, MIX_W), bwd)],
        out_shape=[out, out],
        scratch_shapes=[pltpu.VMEM((2, D_STATE, MIX_W), F32)],
        compiler_params=_cparams(("parallel", "arbitrary")),
        name="ssd_scan",
    )(u, dt_raw, u, dt_raw, dtb, a_row, sel, expand)


def _out_proj(x_ref, mod_ref, att_ref, mix, wo_ref, o_ref):
    out = (jnp.dot(att_ref[0], wo_ref[0:MIX_W, :], preferred_element_type=F32)
           + jnp.dot(mix.astype(BF16), wo_ref[MIX_W:, :], preferred_element_type=F32))
    o_ref[0] = x_ref[0] + mod_ref[0, 2:3, :] * out


def _post_rwkv_kernel(x_ref, mod_ref, att_ref, yf_ref, yb_ref, bonus_ref, g_ref, lnw_ref, lnb_ref,
                      seg_ref, wo_ref, o_ref):
    y = yf_ref[0] + yb_ref[0]
    inv = 1.0 / HEAD_DIM
    mu = _dot_exact_rhs(y, seg_ref[...]) * inv
    yc = y - mu
    var = _dot_exact_rhs(yc * yc, seg_ref[...]) * inv
    yn = yc * lax.rsqrt(var + B_GN_EPS) * lnw_ref[...] + lnb_ref[...]
    _out_proj(x_ref, mod_ref, att_ref, (yn + bonus_ref[0]) * g_ref[0], wo_ref, o_ref)


def _post_ssd_kernel(x_ref, mod_ref, att_ref, yf_ref, yb_ref, u_ref, z_ref, dsk_ref, nw_ref,
                     wo_ref, o_ref):
    z = z_ref[0]
    y = (yf_ref[0] + yb_ref[0] + dsk_ref[...] * u_ref[0]) * (z * _sigmoid(z))
    gl = MIX_W // D_GROUPS
    parts = []
    for g in range(D_GROUPS):
        yg = y[:, gl * g:gl * (g + 1)]
        parts.append(yg * lax.rsqrt(jnp.mean(yg * yg, axis=-1, keepdims=True) + NORM_EPS))
    _out_proj(x_ref, mod_ref, att_ref, jnp.concatenate(parts, axis=1) * nw_ref[...], wo_ref, o_ref)


def _post_call(kind, x_all, mod, att, extras, rows, w_out, n_batch, ctx_len, tm, lat_only):
    bsz, t_all, _ = x_all.shape
    ctx_tiles = ctx_len // tm
    off = ctx_tiles if lat_only else 0
    nt = t_all // tm - off
    tok = lambda b, t: (b, t + off, 0)
    att_off = 0 if att.shape[1] == t_all else -off
    const = lambda b, t: (0, 0)
    in_specs = [pl.BlockSpec((1, tm, D_MODEL), tok),
                _mod_spec(n_batch, ctx_tiles, off),
                pl.BlockSpec((1, tm, MIX_W), lambda b, t: (b, t + off + att_off, 0))]
    in_specs += [pl.BlockSpec((1, tm, MIX_W), tok) for _ in extras]
    in_specs += [pl.BlockSpec(r.shape, const) for r in rows]
    in_specs += [pl.BlockSpec((D_MODEL, D_MODEL), const)]
    body = _post_rwkv_kernel if kind == "rwkv" else _post_ssd_kernel
    return pl.pallas_call(
        body,
        grid=(bsz, nt),
        in_specs=in_specs,
        out_specs=pl.BlockSpec((1, tm, D_MODEL), lambda b, t: (b, t, 0)),
        out_shape=jax.ShapeDtypeStruct((bsz, nt * tm, D_MODEL), F32),
        compiler_params=_cparams(("parallel", "arbitrary")),
        name="post_" + kind,
    )(x_all, mod, att, *extras, *rows, w_out)


def _mlp_kernel(x_ref, mod_ref, nw_ref, w1_ref, w2_ref, fw_ref, o_ref, *, final):
    x = x_ref[0]
    ms = jnp.mean(x * x, axis=-1, keepdims=True)
    h = x * lax.rsqrt(ms + NORM_EPS) * nw_ref[...]
    hb = (h * (1.0 + mod_ref[0, 4:5, :]) + mod_ref[0, 3:4, :]).astype(BF16)
    acc = jnp.zeros_like(x)
    for j in range(MLP_HIDDEN // D_MODEL):
        a = jnp.maximum(jnp.dot(hb, w1_ref[:, D_MODEL * j:D_MODEL * (j + 1)], preferred_element_type=F32), 0.0)
        acc = acc + jnp.dot((a * a).astype(BF16), w2_ref[D_MODEL * j:D_MODEL * (j + 1), :],
                            preferred_element_type=F32)
    x = x + mod_ref[0, 5:6, :] * acc
    if final:
        x = x * lax.rsqrt(jnp.mean(x * x, axis=-1, keepdims=True) + NORM_EPS) * fw_ref[...]
    o_ref[0] = x


def _mlp_call(x_in, mod, norm_w, w1, w2, final_w, n_batch, ctx_tiles, tm, final):
    bsz, t_in, _ = x_in.shape
    const = lambda b, t: (0, 0)
    tok = lambda b, t: (b, t, 0)
    weights = lambda shape: pl.BlockSpec(shape, const, pipeline_mode=pl.Buffered(1))
    return pl.pallas_call(
        functools.partial(_mlp_kernel, final=final),
        grid=(bsz, t_in // tm),
        in_specs=[pl.BlockSpec((1, tm, D_MODEL), tok),
                  _mod_spec(n_batch, ctx_tiles, 0),
                  pl.BlockSpec((1, D_MODEL), const),
                  weights((D_MODEL, MLP_HIDDEN)),
                  weights((MLP_HIDDEN, D_MODEL)),
                  pl.BlockSpec((1, D_MODEL), const)],
        out_specs=pl.BlockSpec((1, tm, D_MODEL), tok),
        out_shape=jax.ShapeDtypeStruct((bsz, t_in, D_MODEL), F32),
        compiler_params=_cparams(("parallel", "arbitrary")),
        name="mlp_final" if final else "mlp",
    )(x_in, mod, norm_w, w1, w2, final_w)


def _rope_tables(n_lat, ctx_len):
    rows = n_lat // GRID_W
    row = jnp.repeat(jnp.arange(rows, dtype=F32), GRID_W)
    col = jnp.tile(jnp.arange(GRID_W, dtype=F32), rows)
    n_freq = HEAD_DIM // 4
    inv_freq = ROPE_THETA ** (-jnp.arange(n_freq, dtype=F32) / n_freq)
    ang_r = row[:, None] * inv_freq
    ang_c = col[:, None] * inv_freq
    cr, sr, cc, sc = jnp.cos(ang_r), jnp.sin(ang_r), jnp.cos(ang_c), jnp.sin(ang_c)
    cos = jnp.concatenate([cr, cr, cc, cc], axis=-1)
    sin = jnp.concatenate([-sr, sr, -sc, sc], axis=-1)
    cos = jnp.concatenate([jnp.ones((ctx_len, HEAD_DIM), F32), cos], axis=0)
    sin = jnp.concatenate([jnp.zeros((ctx_len, HEAD_DIM), F32), sin], axis=0)
    return jnp.tile(cos, (1, N_HEADS)), jnp.tile(sin, (1, N_HEADS))


def _dup_kv_cols(w):
    k0, k1 = w[:, :HEAD_DIM], w[:, HEAD_DIM:]
    return jnp.concatenate([k0, k0, k1, k1], axis=1)


def _seg_matrix():
    return jnp.asarray(np.kron(np.eye(N_HEADS), np.ones((HEAD_DIM, HEAD_DIM))), BF16)


def _tile_rows(ctx_len):
    return 256 if ctx_len % 256 == 0 else 128


def _forward(x, c, ctx, c_ctx, w_mod, b_mod, norm_mix, norm_mlp, w_out, mlp_w1, mlp_w2, final_norm,
             ab_w_in, a_q_norm, a_k_norm, b_mu_prev, b_mu_next, b_w0, b_w2, b_a0, b_a2, b_g2,
             b_k_k, b_k_a, b_r_k, b_ln_w, b_ln_b,
             cd_w_in, c_sink, d_conv_w, d_conv_b, d_dt_bias, d_A_log, d_D, d_norm_w):
    bsz, n_lat, _ = x.shape
    ctx_len = ctx.shape[1]
    tm = _tile_rows(ctx_len)
    ctx_tiles = ctx_len // tm
    mod_rows = -(-(bsz + 1) // 8) * 8
    c_all = jnp.concatenate([c, c_ctx[None, :], jnp.zeros((mod_rows - bsz - 1, D_MODEL), F32)], axis=0)
    mods = _mod_call(c_all, w_mod, b_mod).reshape(DEPTH, mod_rows, 6, D_MODEL)
    cos, sin = _rope_tables(n_lat, ctx_len)
    seg = _seg_matrix()
    row = lambda a: a.reshape(1, -1).astype(F32)
    x_all = jnp.concatenate([ctx, x], axis=1)
    head_dup = lambda w, n: jnp.tile(w.astype(F32), n).reshape(1, -1)

    for i in range(DEPTH):
        j = i // 2
        last = i == DEPTH - 1
        mod = mods[i]
        if i % 2 == 0:
            w = ab_w_in[j]
            w_in = jnp.concatenate([w[:, :MIX_W], _dup_kv_cols(w[:, MIX_W:MIX_W + 128]), w[:, MIX_W + 128:]],
                                   axis=1).astype(BF16)
            q, kdup, vt, f = _premix_call(
                x_all, mod, row(norm_mix[i]), w_in, cos, sin, head_dup(a_q_norm[j], N_HEADS),
                head_dup(a_k_norm[j], 2 * KV_HEADS), seg, (MIX_W, KV_DUP_W, KV_W, B_IN),
                (BF16, BF16, BF16, F32), True, bsz, ctx_tiles, tm)
            att = _gqa_call(q, kdup, vt, ctx_len)
            w2cat = jnp.zeros((B_LORA_W, 2 * MIX_W), F32)
            w2cat = w2cat.at[:64, :MIX_W].set(b_w2[j, 0]).at[64:, MIX_W:].set(b_w2[j, 1])
            a2cat = jnp.zeros((B_LORA_W, 2 * MIX_W), F32)
            a2cat = a2cat.at[:64, :MIX_W].set(b_a2[j, 0]).at[64:, MIX_W:].set(b_a2[j, 1])
            p = dict(mu_prev=row(b_mu_prev[j]), mu_next=row(b_mu_next[j]), w0=row(b_w0[j]),
                     w2=w2cat.astype(BF16), a0=row(b_a0[j]), a2=a2cat.astype(BF16),
                     g2=b_g2[j].astype(BF16), k_k=row(b_k_k[j]), k_a=row(b_k_a[j]), r_k=row(b_r_k[j]),
                     seg=seg)
            r, v, kk, lw, bb, kd, bonus, g = _rwkv_prep_call(f, p, ctx_len, 128)
            yf, yb = _rwkv_scan_call(r, v, kk, lw, bb, kd, ctx_len)
            x_all = _post_call("rwkv", x_all, mod, att, [yf, yb, bonus, g],
                               [row(b_ln_w[j]), row(b_ln_b[j]), seg], w_out[i].astype(BF16),
                               bsz, ctx_len, tm, False)
        else:
            w = cd_w_in[j]
            w_in = jnp.concatenate([w[:, :MIX_W], _dup_kv_cols(w[:, MIX_W:MIX_W + 128]), w[:, MIX_W + 128:],
                                    jnp.zeros((D_MODEL, DT_PAD - 2 * N_HEADS), F32)], axis=1).astype(BF16)
            q, kdup, vt, z, xbc, dt_raw = _premix_call(
                x_all, mod, row(norm_mix[i]), w_in, cos, sin, jnp.ones((1, MIX_W), F32),
                jnp.ones((1, KV_DUP_W), F32), seg, (MIX_W, KV_DUP_W, KV_W, MIX_W, D_XBC, DT_PAD),
                (BF16, BF16, BF16, F32, F32, F32), False, bsz, ctx_tiles, tm)
            att = _swa_call(c_sink[j].astype(F32), q, kdup, vt, ctx_len, last)
            u = _conv_call(xbc, d_conv_w[j], row(d_conv_b[j]), ctx_len, 128)
            pad = jnp.zeros((DT_PAD - 2 * N_HEADS,), F32)
            dtb = jnp.concatenate([d_dt_bias[j].reshape(-1), pad]).reshape(1, -1)
            a_row = jnp.concatenate([-jnp.exp(d_A_log[j].astype(F32)).reshape(-1), pad]).reshape(1, -1)
            sel = np.zeros((2, DT_PAD, DT_PAD), np.float32)
            expand = np.zeros((DT_PAD, MIX_W), np.float32)
            for h in range(N_HEADS):
                sel[0, h, h] = 1.0
                sel[1, N_HEADS + h, h] = 1.0
                expand[h, HEAD_DIM * h:HEAD_DIM * (h + 1)] = 1.0
            yf, yb = _ssd_scan_call(u, dt_raw, dtb, a_row, jnp.asarray(sel, BF16), jnp.asarray(expand, BF16), ctx_len)
            dsk = jnp.repeat(d_D[j].astype(F32), HEAD_DIM).reshape(1, -1)
            x_all = _post_call("ssd", x_all, mod, att, [yf, yb, u, z], [dsk, row(d_norm_w[j])],
                               w_out[i].astype(BF16), bsz, ctx_len, tm, last)
        x_all = _mlp_call(x_all, mod, row(norm_mlp[i]), mlp_w1[i].astype(BF16), mlp_w2[i].astype(BF16),
                          row(final_norm), bsz, 0 if last else ctx_tiles, tm, last)
    return x_all


def kernel(x, c, ctx, c_ctx, w_mod, b_mod, norm_mix, norm_mlp, w_out, mlp_w1, mlp_w2, final_norm, ab_w_in, a_q_norm, a_k_norm, b_mu_prev, b_mu_next, b_w0, b_w2, b_a0, b_a2, b_g2, b_k_k, b_k_a, b_r_k, b_ln_w, b_ln_b, cd_w_in, c_sink, d_conv_w, d_conv_b, d_dt_bias, d_A_log, d_D, d_norm_w):
    return _forward(x, c, ctx, c_ctx, w_mod, b_mod, norm_mix, norm_mlp, w_out, mlp_w1, mlp_w2, final_norm,
                    ab_w_in, a_q_norm, a_k_norm, b_mu_prev, b_mu_next, b_w0, b_w2, b_a0, b_a2, b_g2,
                    b_k_k, b_k_a, b_r_k, b_ln_w, b_ln_b,
                    cd_w_in, c_sink, d_conv_w, d_conv_b, d_dt_bias, d_A_log, d_D, d_norm_w)
```

```python
import functools
import math

import numpy as np
import jax
import jax.numpy as jnp
from jax import lax
from jax.experimental import pallas as pl
from jax.experimental.pallas import tpu as pltpu

F32 = jnp.float32
BF16 = jnp.bfloat16

D_MODEL = 1024
DEPTH = 4
GRID_W = 64
HEAD_DIM = 64
ROPE_THETA = 10000.0
NORM_EPS = 1e-6
MLP_HIDDEN = 4 * D_MODEL
N_HEADS = 8
MIX_W = N_HEADS * HEAD_DIM
KV_HEADS = 2
KV_W = KV_HEADS * HEAD_DIM
KV_DUP_W = 2 * KV_W
B_LORA_W = 128
B_IN = 3 * MIX_W + 3 * B_LORA_W
B_GN_EPS = 64e-5
WINDOW = 128
D_STATE = 128
D_GROUPS = 2
D_CONV = 5
D_XBC = MIX_W + 2 * D_GROUPS * D_STATE
DT_PAD = 128
SSD_CHUNK = 128
RWKV_CHUNK = 64
ATT_TQ = 128
LOG2E = math.log2(math.e)
Q_SCALE = HEAD_DIM ** -0.5 * LOG2E
ONES_ROWS = 16
HALO = 8
HALO_BF16 = 16
VMEM_LIMIT = 56 * 1024 * 1024


def _cparams(sem):
    return pltpu.CompilerParams(dimension_semantics=sem, vmem_limit_bytes=VMEM_LIMIT)


def _dot(a, b):
    return jnp.dot(a.astype(BF16), b.astype(BF16), preferred_element_type=F32)


def _dot_nt(a, b):
    return lax.dot_general(a.astype(BF16), b.astype(BF16), (((1,), (1,)), ((), ())),
                           preferred_element_type=F32)


def _split_hi_lo(x):
    hi = x.astype(BF16)
    lo = (x - hi.astype(F32)).astype(BF16)
    return hi, lo


def _dot_exact_rhs(x, m):
    hi, lo = _split_hi_lo(x)
    return (jnp.dot(hi, m, preferred_element_type=F32)
            + jnp.dot(lo, m, preferred_element_type=F32))


def _dot_exact_lhs(m, x):
    hi, lo = _split_hi_lo(x)
    return (jnp.dot(m, hi, preferred_element_type=F32)
            + jnp.dot(m, lo, preferred_element_type=F32))


def _rope(x, cos, sin_signed):
    w = x.shape[-1]
    lane = lax.broadcasted_iota(jnp.int32, x.shape, 1)
    partner = jnp.where((lane & 16) == 0, pltpu.roll(x, w - 16, 1), pltpu.roll(x, 16, 1))
    return x * cos + partner * sin_signed


def _shift_rows(cur, prev8, next8, s):
    n = cur.shape[0]
    if s == 0:
        return cur
    ri = lax.broadcasted_iota(jnp.int32, (HALO, cur.shape[1]), 0)
    if s > 0:
        rolled = pltpu.roll(cur, s, 0)
        fill = pltpu.roll(prev8, s, 0)
        head = jnp.where(ri < s, fill, rolled[:HALO])
        return jnp.concatenate([head, rolled[HALO:]], axis=0)
    a = -s
    rolled = pltpu.roll(cur, n - a, 0)
    fill = pltpu.roll(next8, HALO - a, 0)
    tail = jnp.where(ri >= HALO - a, fill, rolled[n - HALO:])
    return jnp.concatenate([rolled[:n - HALO], tail], axis=0)


def _sigmoid(x):
    return 1.0 / (1.0 + jnp.exp(-x))


def _softplus(x):
    return jnp.maximum(x, 0.0) + jnp.log(1.0 + jnp.exp(-jnp.abs(x)))


def _mod_kernel(c_ref, w_ref, b_ref, o_ref):
    c = c_ref[...]
    o_ref[0] = _dot(c * _sigmoid(c), w_ref[0]) + b_ref[0]


def _mod_call(c_all, w_mod, b_mod):
    rows = c_all.shape[0]
    n_col = w_mod.shape[2] // D_MODEL
    return pl.pallas_call(
        _mod_kernel,
        grid=(DEPTH, n_col),
        in_specs=[pl.BlockSpec((rows, D_MODEL), lambda i, j: (0, 0)),
                  pl.BlockSpec((1, D_MODEL, D_MODEL), lambda i, j: (i, 0, j)),
                  pl.BlockSpec((1, 1, D_MODEL), lambda i, j: (i, 0, j))],
        out_specs=pl.BlockSpec((1, rows, D_MODEL), lambda i, j: (i, 0, j)),
        out_shape=jax.ShapeDtypeStruct((DEPTH, rows, w_mod.shape[2]), F32),
        compiler_params=_cparams(("arbitrary", "arbitrary")),
        name="modulation",
    )(c_all, w_mod, b_mod.reshape(DEPTH, 1, -1))


def _mod_spec(n_batch, ctx_tiles, tile_off):
    return pl.BlockSpec((1, 6, D_MODEL),
                        lambda b, t: (jnp.where(t + tile_off < ctx_tiles, n_batch, b), 0, 0))


def _head_rms(y, seg):
    w = y.shape[-1]
    ssum = _dot(y * y, seg[:w, :w])
    return y * lax.rsqrt(ssum * (1.0 / HEAD_DIM) + NORM_EPS)


def _premix_kernel(x_ref, mod_ref, nw_ref, w_ref, cos_ref, sin_ref, qnw_ref, knw_ref, seg_ref,
                   *out_refs, widths, qk_norm):
    x = x_ref[0]
    ms = jnp.mean(x * x, axis=-1, keepdims=True)
    h = x * lax.rsqrt(ms + NORM_EPS) * nw_ref[...]
    h = h * (1.0 + mod_ref[0, 1:2, :]) + mod_ref[0, 0:1, :]
    hb = h.astype(BF16)
    off = 0
    for idx, (o_ref, wdt) in enumerate(zip(out_refs, widths)):
        y = jnp.dot(hb, w_ref[:, off:off + wdt], preferred_element_type=F32)
        if idx < 2:
            if qk_norm:
                nw = qnw_ref if idx == 0 else knw_ref
                y = _head_rms(y, seg_ref[...]) * nw[...]
            y = _rope(y, cos_ref[:, :wdt], sin_ref[:, :wdt])
            if idx == 0:
                y = y * Q_SCALE
        if idx == 2:
            o_ref[0, 0] = y.T.astype(o_ref.dtype)
        else:
            o_ref[0] = y.astype(o_ref.dtype)
        off += wdt


def _premix_call(x_all, mod, norm_w, w_in, cos, sin, qnw, knw, seg, widths, dtypes, qk_norm,
                 n_batch, ctx_tiles, tm):
    bsz, t_all, _ = x_all.shape
    nt = t_all // tm
    n_in = w_in.shape[1]
    const = lambda b, t: (0, 0)
    tok = lambda b, t: (b, t, 0)
    out_specs = [pl.BlockSpec((1, tm, w), tok) for w in widths]
    out_shape = [jax.ShapeDtypeStruct((bsz, t_all, w), d) for w, d in zip(widths, dtypes)]
    out_specs[2] = pl.BlockSpec((1, 1, widths[2], tm), lambda b, t: (b, t, 0, 0))
    out_shape[2] = jax.ShapeDtypeStruct((bsz, nt, widths[2], tm), dtypes[2])
    return pl.pallas_call(
        functools.partial(_premix_kernel, widths=widths, qk_norm=qk_norm),
        grid=(bsz, nt),
        in_specs=[pl.BlockSpec((1, tm, D_MODEL), tok),
                  _mod_spec(n_batch, ctx_tiles, 0),
                  pl.BlockSpec((1, D_MODEL), const),
                  pl.BlockSpec((D_MODEL, n_in), const),
                  pl.BlockSpec((tm, MIX_W), lambda b, t: (t, 0)),
                  pl.BlockSpec((tm, MIX_W), lambda b, t: (t, 0)),
                  pl.BlockSpec((1, MIX_W), const),
                  pl.BlockSpec((1, KV_DUP_W), const),
                  pl.BlockSpec((MIX_W, MIX_W), const)],
        out_specs=out_specs,
        out_shape=out_shape,
        compiler_params=_cparams(("parallel", "arbitrary")),
        name="premix_qknorm" if qk_norm else "premix",
    )(x_all, mod, norm_w, w_in, cos, sin, qnw, knw, seg)


NEG_BIG = -1e30
SCORE_LOOKAHEAD = 2


def _masked_q_pairs(q_ref):
    tq = q_ref.shape[1]
    lane = lax.broadcasted_iota(jnp.int32, (tq, 2 * HEAD_DIM), 1)
    pairs = []
    for j in range(N_HEADS // 2):
        slab = q_ref[0, :, 128 * j:128 * j + 128]
        zero = jnp.zeros_like(slab)
        pairs.append(jnp.concatenate([jnp.where(lane < HEAD_DIM, slab, zero),
                                      jnp.where(lane >= HEAD_DIM, slab, zero)], axis=0))
    return pairs


def _fold_rows_max(s):
    m = s[0:8]
    for i in range(1, s.shape[0] // 8):
        m = jnp.maximum(m, s[8 * i:8 * i + 8])
    return m


def _attend(q_ref, o_ref, blocks, sinks):
    tq = q_ref.shape[1]
    n_pairs = N_HEADS // 2
    ppg = n_pairs // KV_HEADS
    q_pairs = _masked_q_pairs(q_ref)
    row = lax.broadcasted_iota(jnp.int32, (HEAD_DIM + ONES_ROWS, 2 * tq), 0)
    if sinks is None:
        m = [jnp.full((1, 2 * tq), NEG_BIG, F32) for _ in range(n_pairs)]
        acc = [jnp.zeros((HEAD_DIM + ONES_ROWS, 2 * tq), F32) for _ in range(n_pairs)]
    else:
        m = list(sinks)
        acc = [jnp.where(row >= HEAD_DIM, 1.0, 0.0) for _ in range(n_pairs)]
    def scores(blk):
        k_of, _, mask = blk
        out = []
        for g in range(KV_HEADS):
            kc = k_of(g)
            for r in range(ppg):
                sc = _dot_nt(kc, q_pairs[g * ppg + r])
                out.append(sc if mask is None else jnp.where(mask, sc, NEG_BIG))
        return out

    ahead = [scores(b) for b in blocks[:SCORE_LOOKAHEAD]]
    for i, (_, vt_of, _) in enumerate(blocks):
        s = ahead.pop(0)
        if i + SCORE_LOOKAHEAD < len(blocks):
            ahead.append(scores(blocks[i + SCORE_LOOKAHEAD]))
        vts = []
        for g in range(KV_HEADS):
            vt = vt_of(g)
            vts.append(jnp.concatenate([vt, jnp.ones((ONES_ROWS, vt.shape[1]), BF16)], axis=0))
        p, alpha = [], []
        for j in range(n_pairs):
            m_new = jnp.maximum(m[j], jnp.max(_fold_rows_max(s[j]), axis=0, keepdims=True))
            alpha.append(jnp.exp2(m[j] - m_new))
            p.append(jnp.exp2(s[j] - m_new).astype(BF16))
            m[j] = m_new
        for j in range(n_pairs):
            acc[j] = alpha[j] * acc[j] + jnp.dot(vts[j // ppg], p[j], preferred_element_type=F32)
    outs = []
    for a in acc:
        o = a[:HEAD_DIM] / a[HEAD_DIM:HEAD_DIM + 1]
        outs += [o[:, :tq], o[:, tq:]]
    o_ref[0] = jnp.concatenate(outs, axis=0).T.astype(o_ref.dtype)


def _gqa_kernel(q_ref, k_ref, vt_ref, o_ref, *, ctx_chunks, ctx_tiles):
    kc_rows = vt_ref.shape[3]

    def blocks(n_chunks):
        return [(lambda g, j=j: k_ref[0, j * kc_rows:(j + 1) * kc_rows, 128 * g:128 * g + 128],
                 lambda g, j=j: vt_ref[0, j, HEAD_DIM * g:HEAD_DIM * (g + 1), :],
                 None) for j in range(n_chunks)]

    t = pl.program_id(1)

    @pl.when(t < ctx_tiles)
    def _():
        _attend(q_ref, o_ref, blocks(ctx_chunks), None)

    @pl.when(t >= ctx_tiles)
    def _():
        _attend(q_ref, o_ref, blocks(vt_ref.shape[1]), None)


def _gqa_call(q, kdup, vt, ctx_len):
    bsz, t_all, _ = q.shape
    _, n_chunks, vw, kc_rows = vt.shape
    tq = ATT_TQ
    return pl.pallas_call(
        functools.partial(_gqa_kernel, ctx_chunks=ctx_len // kc_rows, ctx_tiles=ctx_len // tq),
        grid=(bsz, t_all // tq),
        in_specs=[pl.BlockSpec((1, tq, MIX_W), lambda b, t: (b, t, 0)),
                  pl.BlockSpec((1, t_all, KV_DUP_W), lambda b, t: (b, 0, 0)),
                  pl.BlockSpec((1, n_chunks, vw, kc_rows), lambda b, t: (b, 0, 0, 0))],
        out_specs=pl.BlockSpec((1, tq, MIX_W), lambda b, t: (b, t, 0)),
        out_shape=jax.ShapeDtypeStruct((bsz, t_all, MIX_W), BF16),
        compiler_params=_cparams(("parallel", "arbitrary")),
        name="global_gqa",
    )(q, kdup, vt)


def _swa_kernel(sink_ref, q_ref, kc_ref, vtc_ref, kp_ref, kq_ref, kn_ref, vtp_ref, vtq_ref, vtn_ref,
                o_ref, *, ctx_tiles, tile_off, n_tiles_all):
    tq = q_ref.shape[1]
    t = pl.program_id(1) + tile_off
    lane2 = lax.broadcasted_iota(jnp.int32, (1, 2 * tq), 1)
    sinks = [jnp.where(lane2 < tq, sink_ref[2 * j] * LOG2E, sink_ref[2 * j + 1] * LOG2E)
             for j in range(N_HEADS // 2)]
    kc_rows = vtc_ref.shape[3]
    ctx_blocks = [(lambda g, j=j: kc_ref[0, j * kc_rows:(j + 1) * kc_rows, 128 * g:128 * g + 128],
                   lambda g, j=j: vtc_ref[0, j, HEAD_DIM * g:HEAD_DIM * (g + 1), :],
                   None) for j in range(vtc_ref.shape[1])]

    def near(k_ref, vt_ref, mask):
        return (lambda g: k_ref[0, :, 128 * g:128 * g + 128],
                lambda g: vt_ref[0, 0, HEAD_DIM * g:HEAD_DIM * (g + 1), :], mask)

    @pl.when(t < ctx_tiles)
    def _():
        _attend(q_ref, o_ref, ctx_blocks, sinks)

    @pl.when(t >= ctx_tiles)
    def _():
        key_i = lax.broadcasted_iota(jnp.int32, (tq, 2 * tq), 0)
        qry_i = lax.broadcasted_iota(jnp.int32, (tq, 2 * tq), 1) & (tq - 1)
        lim_prev = jnp.where(t > ctx_tiles, 0, tq)
        lim_next = jnp.where(t < n_tiles_all - 1, 0, tq)
        _attend(q_ref, o_ref,
                ctx_blocks + [near(kp_ref, vtp_ref, key_i - qry_i >= lim_prev),
                              near(kq_ref, vtq_ref, None),
                              near(kn_ref, vtn_ref, qry_i - key_i >= lim_next)], sinks)


def _swa_call(sink, q, kdup, vt, ctx_len, lat_only):
    bsz, t_all, _ = q.shape
    _, _, vw, kc_rows = vt.shape
    tq = ATT_TQ
    per = kc_rows // tq
    ctx_tiles = ctx_len // tq
    n_all = t_all // tq
    off = ctx_tiles if lat_only else 0
    cur = lambda b, t: t + off
    prv = lambda b, t: jnp.maximum(t + off - 1, ctx_tiles)
    nxt = lambda b, t: jnp.minimum(t + off + 1, n_all - 1)
    k_spec = lambda tile: pl.BlockSpec((1, tq, KV_DUP_W), lambda b, t: (b, tile(b, t), 0))
    vt_spec = lambda tile: pl.BlockSpec((1, 1, vw, tq),
                                        lambda b, t: (b, tile(b, t) // per, 0, tile(b, t) % per))
    return pl.pallas_call(
        functools.partial(_swa_kernel, ctx_tiles=ctx_tiles, tile_off=off, n_tiles_all=n_all),
        grid=(bsz, n_all - off),
        in_specs=[pl.BlockSpec(memory_space=pltpu.SMEM),
                  pl.BlockSpec((1, tq, MIX_W), lambda b, t: (b, t + off, 0)),
                  pl.BlockSpec((1, ctx_len, KV_DUP_W), lambda b, t: (b, 0, 0)),
                  pl.BlockSpec((1, ctx_len // kc_rows, vw, kc_rows), lambda b, t: (b, 0, 0, 0)),
                  k_spec(prv), k_spec(cur), k_spec(nxt), vt_spec(prv), vt_spec(cur), vt_spec(nxt)],
        out_specs=pl.BlockSpec((1, tq, MIX_W), lambda b, t: (b, t, 0)),
        out_shape=jax.ShapeDtypeStruct((bsz, t_all - off * tq, MIX_W), BF16),
        compiler_params=_cparams(("parallel", "arbitrary")),
        name="window_gqa",
    )(sink, q, kdup, vt, kdup, kdup, kdup, vt, vt, vt)


def _segment_flags(t, ctx_tiles, n_tiles):
    first = (t == 0) | (t == ctx_tiles)
    last = (t == ctx_tiles - 1) | (t == n_tiles - 1)
    return first, last


def _rwkv_prep_kernel(f_ref, fp_ref, fn_ref, mup_ref, mun_ref, w0_ref, w2_ref, a0_ref, a2_ref, g2_ref,
                      kk_w_ref, ka_ref, rk_ref, seg_ref,
                      r_ref, v_ref, kk_ref, lw_ref, bb_ref, kd_ref, bonus_ref, g_ref,
                      *, ctx_tiles, n_tiles):
    first, last = _segment_flags(pl.program_id(1), ctx_tiles, n_tiles)
    cur = f_ref[0]
    prev8 = jnp.where(first, 0.0, fp_ref[0])
    next8 = jnp.where(last, 0.0, fn_ref[0])
    prv = _shift_rows(cur, prev8, next8, 1)
    nxt = _shift_rows(cur, prev8, next8, -1)
    f = cur + mup_ref[...] * (prv - cur) + mun_ref[...] * (nxt - cur)
    r = f[:, 0:MIX_W]
    k = f[:, MIX_W:2 * MIX_W]
    v = f[:, 2 * MIX_W:3 * MIX_W]
    wd = f[:, 3 * MIX_W:3 * MIX_W + B_LORA_W]
    ad = f[:, 3 * MIX_W + B_LORA_W:3 * MIX_W + 2 * B_LORA_W]
    gd = f[:, 3 * MIX_W + 2 * B_LORA_W:]
    wlog = w0_ref[...] + _dot(jnp.tanh(wd), w2_ref[...])
    lw_ref[0] = -math.exp(-0.5) * _sigmoid(wlog)
    a = _sigmoid(a0_ref[...] + _dot(ad, a2_ref[...]))
    kk = k * kk_w_ref[...]
    kk = kk * lax.rsqrt(_dot(kk * kk, seg_ref[...]) + 1e-12)
    ksum = jnp.zeros_like(k)
    for d in range(2):
        a_d = a[:, d * MIX_W:(d + 1) * MIX_W]
        k_d = k * (1.0 + (a_d - 1.0) * ka_ref[...])
        bb_ref[0, :, d * MIX_W:(d + 1) * MIX_W] = (a_d * kk).astype(bb_ref.dtype)
        kd_ref[0, :, d * MIX_W:(d + 1) * MIX_W] = k_d.astype(kd_ref.dtype)
        ksum = ksum + k_d
    r_ref[0] = r.astype(r_ref.dtype)
    v_ref[0] = v.astype(v_ref.dtype)
    kk_ref[0] = kk.astype(kk_ref.dtype)
    bonus_ref[0] = (_dot_exact_rhs(r * ksum * rk_ref[...], seg_ref[...]) * v).astype(bonus_ref.dtype)
    g_ref[0] = _dot(_sigmoid(gd), g2_ref[...]).astype(g_ref.dtype)


def _rwkv_prep_call(f, p, ctx_len, tm):
    bsz, t_all, _ = f.shape
    nt = t_all // tm
    ctx_tiles = ctx_len // tm
    nb8 = t_all // HALO
    const = lambda b, t: (0, 0)
    tok = lambda b, t: (b, t, 0)
    wide = jax.ShapeDtypeStruct((bsz, t_all, 2 * MIX_W), BF16)
    one = jax.ShapeDtypeStruct((bsz, t_all, MIX_W), BF16)
    logs = jax.ShapeDtypeStruct((bsz, t_all, 2 * MIX_W), F32)
    row = lambda w: pl.BlockSpec((1, w), const)
    return pl.pallas_call(
        functools.partial(_rwkv_prep_kernel, ctx_tiles=ctx_tiles, n_tiles=nt),
        grid=(bsz, nt),
        in_specs=[pl.BlockSpec((1, tm, B_IN), tok),
                  pl.BlockSpec((1, HALO, B_IN), lambda b, t: (b, jnp.maximum(t * (tm // HALO) - 1, 0), 0)),
                  pl.BlockSpec((1, HALO, B_IN), lambda b, t: (b, jnp.minimum((t + 1) * (tm // HALO), nb8 - 1), 0)),
                  row(B_IN), row(B_IN),
                  row(2 * MIX_W), pl.BlockSpec((B_LORA_W, 2 * MIX_W), const),
                  row(2 * MIX_W), pl.BlockSpec((B_LORA_W, 2 * MIX_W), const),
                  pl.BlockSpec((B_LORA_W, MIX_W), const),
                  row(MIX_W), row(MIX_W), row(MIX_W),
                  pl.BlockSpec((MIX_W, MIX_W), const)],
        out_specs=[pl.BlockSpec((1, tm, MIX_W), tok)] * 3 + [pl.BlockSpec((1, tm, 2 * MIX_W), tok)] * 3
                  + [pl.BlockSpec((1, tm, MIX_W), tok)] * 2,
        out_shape=[one, one, one, logs, wide, wide, one, one],
        compiler_params=_cparams(("parallel", "arbitrary")),
        name="rwkv_prep",
    )(f, f, f, p["mu_prev"], p["mu_next"], p["w0"], p["w2"], p["a0"], p["a2"], p["g2"],
      p["k_k"], p["k_a"], p["r_k"], p["seg"])


def _scan_chunk(s, d, n_ctx, n_all):
    bwd = jnp.where(s < n_ctx, n_ctx - 1 - s, n_all - 1 + n_ctx - s)
    return jnp.where(d == 0, s, bwd)


def _rwkv_scan_kernel(rf_ref, vf_ref, kkf_ref, lwf_ref, bbf_ref, kdf_ref,
                      rb_ref, vb_ref, kkb_ref, lwb_ref, bbb_ref, kdb_ref,
                      yf_ref, yb_ref, state_ref):
    c = rf_ref.shape[1]

    @pl.when(pl.program_id(1) == 0)
    def _():
        state_ref[...] = jnp.zeros_like(state_ref)

    ri = lax.broadcasted_iota(jnp.int32, (c, c), 0)
    ci = lax.broadcasted_iota(jnp.int32, (c, c), 1)
    eye = jnp.where(ri == ci, 1.0, 0.0)

    units = []
    dir_refs = ((rf_ref, vf_ref, kkf_ref, lwf_ref, bbf_ref, kdf_ref),
                (rb_ref, vb_ref, kkb_ref, lwb_ref, bbb_ref, kdb_ref))
    for d, (r_ref, v_ref, kk_ref, lw_ref, bb_ref, kd_ref) in enumerate(dir_refs):
        ahead = ri - ci if d == 0 else ci - ri
        incl = ahead >= 0
        strict = ahead > 0
        lw = lw_ref[0]
        cum = _dot_exact_lhs(jnp.where(incl, 1.0, 0.0).astype(BF16), lw)
        e_cum = jnp.exp(cum)
        e_inv = jnp.exp(-cum)
        e_tot = jnp.exp(jnp.sum(lw, axis=0, keepdims=True))
        qt = r_ref[0] * e_cum
        kkt = kk_ref[0] * jnp.exp(cum - lw)
        kh = kd_ref[0] * e_inv
        bh = bb_ref[0] * e_inv
        k_end = kh * e_tot
        b_end = -(bh * e_tot)
        v = v_ref[0].astype(F32)
        for h in range(N_HEADS):
            sl = slice(HEAD_DIM * h, HEAD_DIM * (h + 1))
            units.append(dict(
                d=d, h=h, incl=incl, strict=strict, v=v[:, sl], e_tot=e_tot[:, sl],
                lhs=jnp.concatenate([qt[:, sl], kkt[:, sl]], axis=0).astype(BF16),
                rhs=jnp.concatenate([kh[:, sl], bh[:, sl]], axis=0).astype(BF16),
                x_end=jnp.concatenate([k_end[:, sl], b_end[:, sl]], axis=0).astype(BF16)))

    for u in units:
        a = _dot_nt(u["lhs"], u["rhs"])
        u["a_v"] = jnp.concatenate([jnp.where(u["incl"], a[:c, :c], 0.0),
                                    jnp.where(u["strict"], a[c:, :c], 0.0)], axis=0).astype(BF16)
        u["a_rb"] = jnp.where(u["incl"], a[:c, c:], 0.0).astype(BF16)
        u["p"] = -jnp.where(u["strict"], a[c:, c:], 0.0)
    for u in units:
        u["t"] = eye + u["p"]
        u["p"] = _dot(u["p"], u["p"])
    m = 2
    while 2 * m < c:
        for u in units:
            both = _dot(jnp.concatenate([u["t"], u["p"]], axis=0), u["p"])
            u["t"] = u["t"] + both[:c]
            u["p"] = both[c:]
        m *= 2
    for u in units:
        u["t"] = (u["t"] + _dot(u["t"], u["p"])).astype(BF16)
        u["av"] = _dot(u["a_v"], u["v"])
    for u in units:
        u["st"] = state_ref[u["d"], u["h"]]
        u["x1"] = _dot_nt(u["lhs"], u["st"]) + u["av"]
    for u in units:
        u["u"] = _dot(u["t"], u["x1"][c:])
    ys = ([], [])
    for u in units:
        ys[u["d"]].append(u["x1"][:c] - _dot(u["a_rb"], u["u"]))
        wv = jnp.concatenate([u["v"], u["u"]], axis=0)
        state_ref[u["d"], u["h"]] = u["st"] * u["e_tot"] + _dot(wv.T, u["x_end"])
    yf_ref[0] = jnp.concatenate(ys[0], axis=1)
    yb_ref[0] = jnp.concatenate(ys[1], axis=1)


def _rwkv_scan_call(r, v, kk, lw, bb, kd, ctx_len):
    bsz, t_all, _ = r.shape
    c = RWKV_CHUNK
    n_all = t_all // c
    n_ctx = ctx_len // c
    blk = lambda im: pl.BlockSpec((1, c, MIX_W), im)
    fwd = lambda b, s: (b, s, 0)
    bwd = lambda b, s: (b, _scan_chunk(s, 1, n_ctx, n_all), 0)
    bwd1 = lambda b, s: (b, _scan_chunk(s, 1, n_ctx, n_all), 1)
    out = jax.ShapeDtypeStruct((bsz, t_all, MIX_W), F32)
    return pl.pallas_call(
        _rwkv_scan_kernel,
        grid=(bsz, n_all),
        in_specs=[blk(fwd)] * 6 + [blk(bwd)] * 3 + [blk(bwd1)] * 3,
        out_specs=[blk(fwd), blk(bwd)],
        out_shape=[out, out],
        scratch_shapes=[pltpu.VMEM((2, N_HEADS, HEAD_DIM, HEAD_DIM), F32)],
        compiler_params=_cparams(("parallel", "arbitrary")),
        name="rwkv_scan",
    )(r, v, kk, lw, bb, kd, r, v, kk, lw, bb, kd)


def _conv_kernel(x_ref, xp_ref, xn_ref, w_ref, b_ref, o_ref, *, ctx_tiles, n_tiles):
    first, last = _segment_flags(pl.program_id(1), ctx_tiles, n_tiles)
    cur = x_ref[0].astype(F32)
    prev8 = jnp.where(first, 0.0, xp_ref[0].astype(F32)[HALO_BF16 - HALO:])
    next8 = jnp.where(last, 0.0, xn_ref[0].astype(F32)[:HALO])
    acc = b_ref[...] + w_ref[D_CONV // 2:D_CONV // 2 + 1, :] * cur
    for kpos in range(D_CONV):
        s = D_CONV // 2 - kpos
        if s != 0:
            acc = acc + w_ref[kpos:kpos + 1, :] * _shift_rows(cur, prev8, next8, s)
    o_ref[0] = (acc * _sigmoid(acc)).astype(o_ref.dtype)


def _conv_call(xbc, conv_w, conv_b, ctx_len, tm):
    bsz, t_all, w = xbc.shape
    nt = t_all // tm
    hb = HALO_BF16
    n_halo = t_all // hb
    return pl.pallas_call(
        functools.partial(_conv_kernel, ctx_tiles=ctx_len // tm, n_tiles=nt),
        grid=(bsz, nt),
        in_specs=[pl.BlockSpec((1, tm, w), lambda b, t: (b, t, 0)),
                  pl.BlockSpec((1, hb, w), lambda b, t: (b, jnp.maximum(t * (tm // hb) - 1, 0), 0)),
                  pl.BlockSpec((1, hb, w), lambda b, t: (b, jnp.minimum((t + 1) * (tm // hb), n_halo - 1), 0)),
                  pl.BlockSpec((D_CONV, w), lambda b, t: (0, 0)),
                  pl.BlockSpec((1, w), lambda b, t: (0, 0))],
        out_specs=pl.BlockSpec((1, tm, w), lambda b, t: (b, t, 0)),
        out_shape=jax.ShapeDtypeStruct((bsz, t_all, w), BF16),
        compiler_params=_cparams(("parallel", "arbitrary")),
        name="ssd_conv",
    )(xbc, xbc, xbc, conv_w, conv_b)


def _ssd_scan_kernel(uf_ref, dtf_ref, ub_ref, dtb_ref, bias_ref, a_ref, sel_ref, exp_ref,
                     yf_ref, yb_ref, state_ref):
    c = uf_ref.shape[1]

    @pl.when(pl.program_id(1) == 0)
    def _():
        state_ref[...] = jnp.zeros_like(state_ref)

    ri = lax.broadcasted_iota(jnp.int32, (c, c), 0)
    ci = lax.broadcasted_iota(jnp.int32, (c, c), 1)
    expand = exp_ref[...]
    gw = D_STATE
    gl = MIX_W // D_GROUPS
    hpg = N_HEADS // D_GROUPS
    dirs = []
    for d, (u_ref, dt_ref) in enumerate(((uf_ref, dtf_ref), (ub_ref, dtb_ref))):
        incl = (ri - ci if d == 0 else ci - ri) >= 0
        incl_m = jnp.where(incl, 1.0, 0.0).astype(BF16)
        dt_all = _softplus(dt_ref[0] + bias_ref[...])
        dt_d = _dot_exact_rhs(dt_all, sel_ref[d])
        dta_d = _dot_exact_rhs(dt_all * a_ref[...], sel_ref[d])
        cum_d = _dot_exact_lhs(incl_m, dta_d)
        dta_x = _dot_exact_rhs(dta_d, expand)
        dirs.append(dict(u_ref=u_ref, incl=incl, cum_d=cum_d, cum_t=cum_d.T, dt_t=dt_d.T,
                         cum_x=_dot_exact_lhs(incl_m, dta_x), dt_x=_dot_exact_rhs(dt_d, expand),
                         tot_x=jnp.sum(dta_x, axis=0, keepdims=True), xs=u_ref[0, :, 0:MIX_W], ys=[]))

    def b_mat(u_ref, g):
        return u_ref[0, :, MIX_W + gw * g:MIX_W + gw * (g + 1)]

    def c_mat(u_ref, g):
        return u_ref[0, :, MIX_W + gw * D_GROUPS + gw * g:MIX_W + gw * D_GROUPS + gw * (g + 1)]

    for g in range(D_GROUPS):
        for p in dirs:
            p["cb"] = _dot_nt(c_mat(p["u_ref"], g), b_mat(p["u_ref"], g))
        for r in range(hpg):
            h = g * hpg + r
            for p in dirs:
                seg = p["cum_d"][:, h:h + 1] - p["cum_t"][h:h + 1, :]
                lmat = jnp.exp(jnp.where(p["incl"], seg, -jnp.inf))
                mm = p["cb"] * lmat * p["dt_t"][h:h + 1, :]
                p["ys"].append(_dot(mm, p["xs"][:, HEAD_DIM * h:HEAD_DIM * (h + 1)]))

    for d, (p, y_ref) in enumerate(zip(dirs, (yf_ref, yb_ref))):
        wj = jnp.exp(p["tot_x"] - p["cum_x"]) * p["dt_x"]
        xw = p["xs"] * wj
        e_tot = jnp.exp(p["tot_x"])
        offs = []
        for g in range(D_GROUPS):
            st = state_ref[d, :, gl * g:gl * (g + 1)]
            offs.append(_dot(c_mat(p["u_ref"], g), st))
            bm_t = b_mat(p["u_ref"], g).astype(F32).T
            state_ref[d, :, gl * g:gl * (g + 1)] = (st * e_tot[:, gl * g:gl * (g + 1)]
                                                    + _dot(bm_t, xw[:, gl * g:gl * (g + 1)]))
        y_ref[0] = (jnp.concatenate(p["ys"], axis=1)
                    + jnp.concatenate(offs, axis=1) * jnp.exp(p["cum_x"]))


def _ssd_scan_call(u, dt_raw, dtb, a_row, sel, expand, ctx_len):
    bsz, t_all, w = u.shape
    c = SSD_CHUNK
    n_all = t_all // c
    n_ctx = ctx_len // c
    fwd = lambda b, s: (b, s, 0)
    bwd = lambda b, s: (b, _scan_chunk(s, 1, n_ctx, n_all), 0)
    const = lambda b, s: (0, 0)
    out = jax.ShapeDtypeStruct((bsz, t_all, MIX_W), F32)
    return pl.pallas_call(
        _ssd_scan_kernel,
        grid=(bsz, n_all),
        in_specs=[pl.BlockSpec((1, c, w), fwd), pl.BlockSpec((1, c, DT_PAD), fwd),
                  pl.BlockSpec((1, c, w), bwd), pl.BlockSpec((1, c, DT_PAD), bwd),
                  pl.BlockSpec((1, DT_PAD), const),
                  pl.BlockSpec((1, DT_PAD), const),
                  pl.BlockSpec((2, DT_PAD, DT_PAD), lambda b, s: (0, 0, 0)),
                  pl.BlockSpec((DT_PAD, MIX_W), const)],
        out_specs=[pl.BlockSpec((1, c, MIX_W), fwd), pl.BlockSpec((1, c, MIX_W), bwd)],
        out_shape=[out, out],
        scratch_shapes=[pltpu.VMEM((2, D_STATE, MIX_W), F32)],
        compiler_params=_cparams(("parallel", "arbitrary")),
        name="ssd_scan",
    )(u, dt_raw, u, dt_raw, dtb, a_row, sel, expand)


def _out_proj(x_ref, mod_ref, att_ref, mix, wo_ref, o_ref):
    out = (jnp.dot(att_ref[0], wo_ref[0:MIX_W, :], preferred_element_type=F32)
           + jnp.dot(mix.astype(BF16), wo_ref[MIX_W:, :], preferred_element_type=F32))
    o_ref[0] = x_ref[0] + mod_ref[0, 2:3, :] * out


def _post_rwkv_kernel(x_ref, mod_ref, att_ref, yf_ref, yb_ref, bonus_ref, g_ref, lnw_ref, lnb_ref,
                      seg_ref, wo_ref, o_ref):
    y = yf_ref[0] + yb_ref[0]
    inv = 1.0 / HEAD_DIM
    mu = _dot_exact_rhs(y, seg_ref[...]) * inv
    yc = y - mu
    var = _dot(yc * yc, seg_ref[...]) * inv
    yn = yc * lax.rsqrt(var + B_GN_EPS) * lnw_ref[...] + lnb_ref[...]
    _out_proj(x_ref, mod_ref, att_ref, (yn + bonus_ref[0]) * g_ref[0], wo_ref, o_ref)


def _post_ssd_kernel(x_ref, mod_ref, att_ref, yf_ref, yb_ref, u_ref, z_ref, dsk_ref, nw_ref,
                     wo_ref, o_ref):
    z = z_ref[0].astype(F32)
    y = (yf_ref[0] + yb_ref[0] + dsk_ref[...] * u_ref[0].astype(F32)) * (z * _sigmoid(z))
    gl = MIX_W // D_GROUPS
    parts = []
    for g in range(D_GROUPS):
        yg = y[:, gl * g:gl * (g + 1)]
        parts.append(yg * lax.rsqrt(jnp.mean(yg * yg, axis=-1, keepdims=True) + NORM_EPS))
    _out_proj(x_ref, mod_ref, att_ref, jnp.concatenate(parts, axis=1) * nw_ref[...], wo_ref, o_ref)


def _post_call(kind, x_all, mod, att, extras, rows, w_out, n_batch, ctx_len, tm, lat_only):
    bsz, t_all, _ = x_all.shape
    ctx_tiles = ctx_len // tm
    off = ctx_tiles if lat_only else 0
    nt = t_all // tm - off
    tok = lambda b, t: (b, t + off, 0)
    att_off = 0 if att.shape[1] == t_all else -off
    const = lambda b, t: (0, 0)
    in_specs = [pl.BlockSpec((1, tm, D_MODEL), tok),
                _mod_spec(n_batch, ctx_tiles, off),
                pl.BlockSpec((1, tm, MIX_W), lambda b, t: (b, t + off + att_off, 0))]
    in_specs += [pl.BlockSpec((1, tm, MIX_W), tok) for _ in extras]
    in_specs += [pl.BlockSpec(r.shape, const) for r in rows]
    in_specs += [pl.BlockSpec((D_MODEL, D_MODEL), const)]
    body = _post_rwkv_kernel if kind == "rwkv" else _post_ssd_kernel
    return pl.pallas_call(
        body,
        grid=(bsz, nt),
        in_specs=in_specs,
        out_specs=pl.BlockSpec((1, tm, D_MODEL), lambda b, t: (b, t, 0)),
        out_shape=jax.ShapeDtypeStruct((bsz, nt * tm, D_MODEL), F32),
        compiler_params=_cparams(("parallel", "arbitrary")),
        name="post_" + kind,
    )(x_all, mod, att, *extras, *rows, w_out)


def _mlp_kernel(x_ref, mod_ref, nw_ref, w1_ref, w2_ref, fw_ref, o_ref, *, final):
    x = x_ref[0]
    ms = jnp.mean(x * x, axis=-1, keepdims=True)
    h = x * lax.rsqrt(ms + NORM_EPS) * nw_ref[...]
    hb = (h * (1.0 + mod_ref[0, 4:5, :]) + mod_ref[0, 3:4, :]).astype(BF16)
    acc = jnp.zeros_like(x)
    for j in range(MLP_HIDDEN // D_MODEL):
        a = jnp.maximum(jnp.dot(hb, w1_ref[:, D_MODEL * j:D_MODEL * (j + 1)], preferred_element_type=F32), 0.0)
        acc = acc + jnp.dot((a * a).astype(BF16), w2_ref[D_MODEL * j:D_MODEL * (j + 1), :],
                            preferred_element_type=F32)
    x = x + mod_ref[0, 5:6, :] * acc
    if final:
        x = x * lax.rsqrt(jnp.mean(x * x, axis=-1, keepdims=True) + NORM_EPS) * fw_ref[...]
    o_ref[0] = x


def _mlp_call(x_in, mod, norm_w, w1, w2, final_w, n_batch, ctx_tiles, tm, final):
    bsz, t_in, _ = x_in.shape
    const = lambda b, t: (0, 0)
    tok = lambda b, t: (b, t, 0)
    weights = lambda shape: pl.BlockSpec(shape, const, pipeline_mode=pl.Buffered(1))
    return pl.pallas_call(
        functools.partial(_mlp_kernel, final=final),
        grid=(bsz, t_in // tm),
        in_specs=[pl.BlockSpec((1, tm, D_MODEL), tok),
                  _mod_spec(n_batch, ctx_tiles, 0),
                  pl.BlockSpec((1, D_MODEL), const),
                  weights((D_MODEL, MLP_HIDDEN)),
                  weights((MLP_HIDDEN, D_MODEL)),
                  pl.BlockSpec((1, D_MODEL), const)],
        out_specs=pl.BlockSpec((1, tm, D_MODEL), tok),
        out_shape=jax.ShapeDtypeStruct((bsz, t_in, D_MODEL), F32),
        compiler_params=_cparams(("parallel", "arbitrary")),
        name="mlp_final" if final else "mlp",
    )(x_in, mod, norm_w, w1, w2, final_w)


def _rope_tables(n_lat, ctx_len):
    rows = n_lat // GRID_W
    row = jnp.repeat(jnp.arange(rows, dtype=F32), GRID_W)
    col = jnp.tile(jnp.arange(GRID_W, dtype=F32), rows)
    n_freq = HEAD_DIM // 4
    inv_freq = ROPE_THETA ** (-jnp.arange(n_freq, dtype=F32) / n_freq)
    ang_r = row[:, None] * inv_freq
    ang_c = col[:, None] * inv_freq
    cr, sr, cc, sc = jnp.cos(ang_r), jnp.sin(ang_r), jnp.cos(ang_c), jnp.sin(ang_c)
    cos = jnp.concatenate([cr, cr, cc, cc], axis=-1)
    sin = jnp.concatenate([-sr, sr, -sc, sc], axis=-1)
    cos = jnp.concatenate([jnp.ones((ctx_len, HEAD_DIM), F32), cos], axis=0)
    sin = jnp.concatenate([jnp.zeros((ctx_len, HEAD_DIM), F32), sin], axis=0)
    return jnp.tile(cos, (1, N_HEADS)), jnp.tile(sin, (1, N_HEADS))


def _dup_kv_cols(w):
    k0, k1 = w[:, :HEAD_DIM], w[:, HEAD_DIM:]
    return jnp.concatenate([k0, k0, k1, k1], axis=1)


def _seg_matrix():
    return jnp.asarray(np.kron(np.eye(N_HEADS), np.ones((HEAD_DIM, HEAD_DIM))), BF16)


def _tile_rows(ctx_len):
    return 256 if ctx_len % 256 == 0 else 128


def _forward(x, c, ctx, c_ctx, w_mod, b_mod, norm_mix, norm_mlp, w_out, mlp_w1, mlp_w2, final_norm,
             ab_w_in, a_q_norm, a_k_norm, b_mu_prev, b_mu_next, b_w0, b_w2, b_a0, b_a2, b_g2,
             b_k_k, b_k_a, b_r_k, b_ln_w, b_ln_b,
             cd_w_in, c_sink, d_conv_w, d_conv_b, d_dt_bias, d_A_log, d_D, d_norm_w):
    bsz, n_lat, _ = x.shape
    ctx_len = ctx.shape[1]
    tm = _tile_rows(ctx_len)
    ctx_tiles = ctx_len // tm
    mod_rows = -(-(bsz + 1) // 8) * 8
    c_all = jnp.concatenate([c, c_ctx[None, :], jnp.zeros((mod_rows - bsz - 1, D_MODEL), F32)], axis=0)
    mods = _mod_call(c_all, w_mod, b_mod).reshape(DEPTH, mod_rows, 6, D_MODEL)
    cos, sin = _rope_tables(n_lat, ctx_len)
    seg = _seg_matrix()
    row = lambda a: a.reshape(1, -1).astype(F32)
    x_all = jnp.concatenate([ctx, x], axis=1)
    head_dup = lambda w, n: jnp.tile(w.astype(F32), n).reshape(1, -1)

    for i in range(DEPTH):
        j = i // 2
        last = i == DEPTH - 1
        mod = mods[i]
        if i % 2 == 0:
            w = ab_w_in[j]
            w_in = jnp.concatenate([w[:, :MIX_W], _dup_kv_cols(w[:, MIX_W:MIX_W + 128]), w[:, MIX_W + 128:]],
                                   axis=1).astype(BF16)
            q, kdup, vt, f = _premix_call(
                x_all, mod, row(norm_mix[i]), w_in, cos, sin, head_dup(a_q_norm[j], N_HEADS),
                head_dup(a_k_norm[j], 2 * KV_HEADS), seg, (MIX_W, KV_DUP_W, KV_W, B_IN),
                (BF16, BF16, BF16, F32), True, bsz, ctx_tiles, tm)
            att = _gqa_call(q, kdup, vt, ctx_len)
            w2cat = jnp.zeros((B_LORA_W, 2 * MIX_W), F32)
            w2cat = w2cat.at[:64, :MIX_W].set(b_w2[j, 0]).at[64:, MIX_W:].set(b_w2[j, 1])
            a2cat = jnp.zeros((B_LORA_W, 2 * MIX_W), F32)
            a2cat = a2cat.at[:64, :MIX_W].set(b_a2[j, 0]).at[64:, MIX_W:].set(b_a2[j, 1])
            p = dict(mu_prev=row(b_mu_prev[j]), mu_next=row(b_mu_next[j]), w0=row(b_w0[j]),
                     w2=w2cat.astype(BF16), a0=row(b_a0[j]), a2=a2cat.astype(BF16),
                     g2=b_g2[j].astype(BF16), k_k=row(b_k_k[j]), k_a=row(b_k_a[j]), r_k=row(b_r_k[j]),
                     seg=seg)
            r, v, kk, lw, bb, kd, bonus, g = _rwkv_prep_call(f, p, ctx_len, tm)
            yf, yb = _rwkv_scan_call(r, v, kk, lw, bb, kd, ctx_len)
            x_all = _post_call("rwkv", x_all, mod, att, [yf, yb, bonus, g],
                               [row(b_ln_w[j]), row(b_ln_b[j]), seg], w_out[i].astype(BF16),
                               bsz, ctx_len, tm, False)
        else:
            w = cd_w_in[j]
            w_in = jnp.concatenate([w[:, :MIX_W], _dup_kv_cols(w[:, MIX_W:MIX_W + 128]), w[:, MIX_W + 128:],
                                    jnp.zeros((D_MODEL, DT_PAD - 2 * N_HEADS), F32)], axis=1).astype(BF16)
            q, kdup, vt, z, xbc, dt_raw = _premix_call(
                x_all, mod, row(norm_mix[i]), w_in, cos, sin, jnp.ones((1, MIX_W), F32),
                jnp.ones((1, KV_DUP_W), F32), seg, (MIX_W, KV_DUP_W, KV_W, MIX_W, D_XBC, DT_PAD),
                (BF16, BF16, BF16, BF16, BF16, F32), False, bsz, ctx_tiles, tm)
            att = _swa_call(c_sink[j].astype(F32), q, kdup, vt, ctx_len, last)
            u = _conv_call(xbc, d_conv_w[j], row(d_conv_b[j]), ctx_len, tm)
            pad = jnp.zeros((DT_PAD - 2 * N_HEADS,), F32)
            dtb = jnp.concatenate([d_dt_bias[j].reshape(-1), pad]).reshape(1, -1)
            a_row = jnp.concatenate([-jnp.exp(d_A_log[j].astype(F32)).reshape(-1), pad]).reshape(1, -1)
            sel = np.zeros((2, DT_PAD, DT_PAD), np.float32)
            expand = np.zeros((DT_PAD, MIX_W), np.float32)
            for h in range(N_HEADS):
                sel[0, h, h] = 1.0
                sel[1, N_HEADS + h, h] = 1.0
                expand[h, HEAD_DIM * h:HEAD_DIM * (h + 1)] = 1.0
            yf, yb = _ssd_scan_call(u, dt_raw, dtb, a_row, jnp.asarray(sel, BF16), jnp.asarray(expand, BF16), ctx_len)
            dsk = jnp.repeat(d_D[j].astype(F32), HEAD_DIM).reshape(1, -1)
            x_all = _post_call("ssd", x_all, mod, att, [yf, yb, u, z], [dsk, row(d_norm_w[j])],
                               w_out[i].astype(BF16), bsz, ctx_len, tm, last)
        x_all = _mlp_call(x_all, mod, row(norm_mlp[i]), mlp_w1[i].astype(BF16), mlp_w2[i].astype(BF16),
                          row(final_norm), bsz, 0 if last else ctx_tiles, tm, last)
    return x_all


def kernel(x, c, ctx, c_ctx, w_mod, b_mod, norm_mix, norm_mlp, w_out, mlp_w1, mlp_w2, final_norm, ab_w_in, a_q_norm, a_k_norm, b_mu_prev, b_mu_next, b_w0, b_w2, b_a0, b_a2, b_g2, b_k_k, b_k_a, b_r_k, b_ln_w, b_ln_b, cd_w_in, c_sink, d_conv_w, d_conv_b, d_dt_bias, d_A_log, d_D, d_norm_w):
    return _forward(x, c, ctx, c_ctx, w_mod, b_mod, norm_mix, norm_mlp, w_out, mlp_w1, mlp_w2, final_norm,
                    ab_w_in, a_q_norm, a_k_norm, b_mu_prev, b_mu_next, b_w0, b_w2, b_a0, b_a2, b_g2,
                    b_k_k, b_k_a, b_r_k, b_ln_w, b_ln_b,
                    cd_w_in, c_sink, d_conv_w, d_conv_b, d_dt_bias, d_A_log, d_D, d_norm_w)
```

```python
import functools
import math

import numpy as np
import jax
import jax.numpy as jnp
from jax import lax
from jax.experimental import pallas as pl
from jax.experimental.pallas import tpu as pltpu

F32 = jnp.float32
BF16 = jnp.bfloat16

D_MODEL = 1024
DEPTH = 4
GRID_W = 64
HEAD_DIM = 64
ROPE_THETA = 10000.0
NORM_EPS = 1e-6
MLP_HIDDEN = 4 * D_MODEL
N_HEADS = 8
MIX_W = N_HEADS * HEAD_DIM
KV_HEADS = 2
KV_W = KV_HEADS * HEAD_DIM
KV_DUP_W = 2 * KV_W
B_LORA_W = 128
B_IN = 3 * MIX_W + 3 * B_LORA_W
B_GN_EPS = 64e-5
WINDOW = 128
D_STATE = 128
D_GROUPS = 2
D_CONV = 5
D_XBC = MIX_W + 2 * D_GROUPS * D_STATE
DT_PAD = 128
SSD_CHUNK = 128
SSD_SAMPLES_PER_STEP = 2
RWKV_CHUNK = 64
RWKV_SAMPLES_PER_STEP = 4
ATT_TQ = 128
LOG2E = math.log2(math.e)
Q_SCALE = HEAD_DIM ** -0.5 * LOG2E
ONES_ROWS = 16
HALO = 8
HALO_BF16 = 16
VMEM_LIMIT = 56 * 1024 * 1024


def _cparams(sem):
    return pltpu.CompilerParams(dimension_semantics=sem, vmem_limit_bytes=VMEM_LIMIT)


def _dot(a, b):
    return jnp.dot(a.astype(BF16), b.astype(BF16), preferred_element_type=F32)


def _dot_nt(a, b):
    return lax.dot_general(a.astype(BF16), b.astype(BF16), (((1,), (1,)), ((), ())),
                           preferred_element_type=F32)


def _split_hi_lo(x):
    hi = x.astype(BF16)
    lo = (x - hi.astype(F32)).astype(BF16)
    return hi, lo


def _dot_exact_rhs(x, m):
    hi, lo = _split_hi_lo(x)
    return (jnp.dot(hi, m, preferred_element_type=F32)
            + jnp.dot(lo, m, preferred_element_type=F32))


def _dot_exact_lhs(m, x):
    hi, lo = _split_hi_lo(x)
    return (jnp.dot(m, hi, preferred_element_type=F32)
            + jnp.dot(m, lo, preferred_element_type=F32))


def _rope(x, cos, sin_signed):
    w = x.shape[-1]
    lane = lax.broadcasted_iota(jnp.int32, x.shape, 1)
    partner = jnp.where((lane & 16) == 0, pltpu.roll(x, w - 16, 1), pltpu.roll(x, 16, 1))
    return x * cos + partner * sin_signed


def _shift_rows(cur, prev8, next8, s):
    n = cur.shape[0]
    if s == 0:
        return cur
    ri = lax.broadcasted_iota(jnp.int32, (HALO, cur.shape[1]), 0)
    if s > 0:
        rolled = pltpu.roll(cur, s, 0)
        fill = pltpu.roll(prev8, s, 0)
        head = jnp.where(ri < s, fill, rolled[:HALO])
        return jnp.concatenate([head, rolled[HALO:]], axis=0)
    a = -s
    rolled = pltpu.roll(cur, n - a, 0)
    fill = pltpu.roll(next8, HALO - a, 0)
    tail = jnp.where(ri >= HALO - a, fill, rolled[n - HALO:])
    return jnp.concatenate([rolled[:n - HALO], tail], axis=0)


def _sigmoid(x):
    return 1.0 / (1.0 + jnp.exp(-x))


def _softplus(x):
    return jnp.maximum(x, 0.0) + jnp.log(1.0 + jnp.exp(-jnp.abs(x)))


def _mod_kernel(c_ref, w_ref, b_ref, o_ref):
    c = c_ref[...]
    o_ref[0] = _dot(c * _sigmoid(c), w_ref[0]) + b_ref[0]


def _mod_call(c_all, w_mod, b_mod):
    rows = c_all.shape[0]
    n_col = w_mod.shape[2] // D_MODEL
    return pl.pallas_call(
        _mod_kernel,
        grid=(DEPTH, n_col),
        in_specs=[pl.BlockSpec((rows, D_MODEL), lambda i, j: (0, 0)),
                  pl.BlockSpec((1, D_MODEL, D_MODEL), lambda i, j: (i, 0, j)),
                  pl.BlockSpec((1, 1, D_MODEL), lambda i, j: (i, 0, j))],
        out_specs=pl.BlockSpec((1, rows, D_MODEL), lambda i, j: (i, 0, j)),
        out_shape=jax.ShapeDtypeStruct((DEPTH, rows, w_mod.shape[2]), F32),
        compiler_params=_cparams(("arbitrary", "arbitrary")),
        name="modulation",
    )(c_all, w_mod, b_mod.reshape(DEPTH, 1, -1))


def _mod_spec(n_batch, ctx_tiles, tile_off):
    return pl.BlockSpec((1, 6, D_MODEL),
                        lambda b, t: (jnp.where(t + tile_off < ctx_tiles, n_batch, b), 0, 0))


def _head_rms(y, seg):
    w = y.shape[-1]
    ssum = _dot(y * y, seg[:w, :w])
    return y * lax.rsqrt(ssum * (1.0 / HEAD_DIM) + NORM_EPS)


def _premix_kernel(x_ref, mod_ref, nw_ref, w_ref, cos_ref, sin_ref, qnw_ref, knw_ref, seg_ref,
                   *out_refs, widths, qk_norm):
    x = x_ref[0]
    ms = jnp.mean(x * x, axis=-1, keepdims=True)
    h = x * lax.rsqrt(ms + NORM_EPS) * nw_ref[...]
    h = h * (1.0 + mod_ref[0, 1:2, :]) + mod_ref[0, 0:1, :]
    hb = h.astype(BF16)
    off = 0
    for idx, (o_ref, wdt) in enumerate(zip(out_refs, widths)):
        y = jnp.dot(hb, w_ref[:, off:off + wdt], preferred_element_type=F32)
        if idx < 2:
            if qk_norm:
                nw = qnw_ref if idx == 0 else knw_ref
                y = _head_rms(y, seg_ref[...]) * nw[...]
            y = _rope(y, cos_ref[:, :wdt], sin_ref[:, :wdt])
            if idx == 0:
                y = y * Q_SCALE
        if idx == 2:
            o_ref[0, 0] = y.T.astype(o_ref.dtype)
        else:
            o_ref[0] = y.astype(o_ref.dtype)
        off += wdt


def _premix_call(x_all, mod, norm_w, w_in, cos, sin, qnw, knw, seg, widths, dtypes, qk_norm,
                 n_batch, ctx_tiles, tm):
    bsz, t_all, _ = x_all.shape
    nt = t_all // tm
    n_in = w_in.shape[1]
    const = lambda b, t: (0, 0)
    tok = lambda b, t: (b, t, 0)
    out_specs = [pl.BlockSpec((1, tm, w), tok) for w in widths]
    out_shape = [jax.ShapeDtypeStruct((bsz, t_all, w), d) for w, d in zip(widths, dtypes)]
    out_specs[2] = pl.BlockSpec((1, 1, widths[2], tm), lambda b, t: (b, t, 0, 0))
    out_shape[2] = jax.ShapeDtypeStruct((bsz, nt, widths[2], tm), dtypes[2])
    return pl.pallas_call(
        functools.partial(_premix_kernel, widths=widths, qk_norm=qk_norm),
        grid=(bsz, nt),
        in_specs=[pl.BlockSpec((1, tm, D_MODEL), tok),
                  _mod_spec(n_batch, ctx_tiles, 0),
                  pl.BlockSpec((1, D_MODEL), const),
                  pl.BlockSpec((D_MODEL, n_in), const),
                  pl.BlockSpec((tm, MIX_W), lambda b, t: (t, 0)),
                  pl.BlockSpec((tm, MIX_W), lambda b, t: (t, 0)),
                  pl.BlockSpec((1, MIX_W), const),
                  pl.BlockSpec((1, KV_DUP_W), const),
                  pl.BlockSpec((MIX_W, MIX_W), const)],
        out_specs=out_specs,
        out_shape=out_shape,
        compiler_params=_cparams(("parallel", "arbitrary")),
        name="premix_qknorm" if qk_norm else "premix",
    )(x_all, mod, norm_w, w_in, cos, sin, qnw, knw, seg)


NEG_BIG = -1e30
SCORE_LOOKAHEAD = 2


def _masked_q_pairs(q_ref):
    tq = q_ref.shape[1]
    lane = lax.broadcasted_iota(jnp.int32, (tq, 2 * HEAD_DIM), 1)
    pairs = []
    for j in range(N_HEADS // 2):
        slab = q_ref[0, :, 128 * j:128 * j + 128]
        zero = jnp.zeros_like(slab)
        pairs.append(jnp.concatenate([jnp.where(lane < HEAD_DIM, slab, zero),
                                      jnp.where(lane >= HEAD_DIM, slab, zero)], axis=0))
    return pairs


def _fold_rows_max(s):
    m = s[0:8]
    for i in range(1, s.shape[0] // 8):
        m = jnp.maximum(m, s[8 * i:8 * i + 8])
    return m


def _attend(q_ref, o_ref, blocks, sinks):
    tq = q_ref.shape[1]
    n_pairs = N_HEADS // 2
    ppg = n_pairs // KV_HEADS
    q_pairs = _masked_q_pairs(q_ref)
    row = lax.broadcasted_iota(jnp.int32, (HEAD_DIM + ONES_ROWS, 2 * tq), 0)
    if sinks is None:
        m = [jnp.full((1, 2 * tq), NEG_BIG, F32) for _ in range(n_pairs)]
        acc = [jnp.zeros((HEAD_DIM + ONES_ROWS, 2 * tq), F32) for _ in range(n_pairs)]
    else:
        m = list(sinks)
        acc = [jnp.where(row >= HEAD_DIM, 1.0, 0.0) for _ in range(n_pairs)]
    def scores(blk):
        k_of, _, mask = blk
        out = []
        for g in range(KV_HEADS):
            kc = k_of(g)
            for r in range(ppg):
                sc = _dot_nt(kc, q_pairs[g * ppg + r])
                out.append(sc if mask is None else jnp.where(mask, sc, NEG_BIG))
        return out

    ahead = [scores(b) for b in blocks[:SCORE_LOOKAHEAD]]
    for i, (_, vt_of, _) in enumerate(blocks):
        s = ahead.pop(0)
        if i + SCORE_LOOKAHEAD < len(blocks):
            ahead.append(scores(blocks[i + SCORE_LOOKAHEAD]))
        vts = []
        for g in range(KV_HEADS):
            vt = vt_of(g)
            vts.append(jnp.concatenate([vt, jnp.ones((ONES_ROWS, vt.shape[1]), BF16)], axis=0))
        p, alpha = [], []
        for j in range(n_pairs):
            m_new = jnp.maximum(m[j], jnp.max(_fold_rows_max(s[j]), axis=0, keepdims=True))
            alpha.append(jnp.exp2(m[j] - m_new))
            p.append(jnp.exp2(s[j] - m_new).astype(BF16))
            m[j] = m_new
        for j in range(n_pairs):
            acc[j] = alpha[j] * acc[j] + jnp.dot(vts[j // ppg], p[j], preferred_element_type=F32)
    outs = []
    for a in acc:
        o = a[:HEAD_DIM] / a[HEAD_DIM:HEAD_DIM + 1]
        outs += [o[:, :tq], o[:, tq:]]
    o_ref[0] = jnp.concatenate(outs, axis=0).T.astype(o_ref.dtype)


def _gqa_kernel(q_ref, k_ref, vt_ref, o_ref, *, ctx_chunks, ctx_tiles):
    kc_rows = vt_ref.shape[3]

    def blocks(n_chunks):
        return [(lambda g, j=j: k_ref[0, j * kc_rows:(j + 1) * kc_rows, 128 * g:128 * g + 128],
                 lambda g, j=j: vt_ref[0, j, HEAD_DIM * g:HEAD_DIM * (g + 1), :],
                 None) for j in range(n_chunks)]

    t = pl.program_id(1)

    @pl.when(t < ctx_tiles)
    def _():
        _attend(q_ref, o_ref, blocks(ctx_chunks), None)

    @pl.when(t >= ctx_tiles)
    def _():
        _attend(q_ref, o_ref, blocks(vt_ref.shape[1]), None)


def _gqa_call(q, kdup, vt, ctx_len):
    bsz, t_all, _ = q.shape
    _, n_chunks, vw, kc_rows = vt.shape
    tq = ATT_TQ
    return pl.pallas_call(
        functools.partial(_gqa_kernel, ctx_chunks=ctx_len // kc_rows, ctx_tiles=ctx_len // tq),
        grid=(bsz, t_all // tq),
        in_specs=[pl.BlockSpec((1, tq, MIX_W), lambda b, t: (b, t, 0)),
                  pl.BlockSpec((1, t_all, KV_DUP_W), lambda b, t: (b, 0, 0)),
                  pl.BlockSpec((1, n_chunks, vw, kc_rows), lambda b, t: (b, 0, 0, 0))],
        out_specs=pl.BlockSpec((1, tq, MIX_W), lambda b, t: (b, t, 0)),
        out_shape=jax.ShapeDtypeStruct((bsz, t_all, MIX_W), BF16),
        compiler_params=_cparams(("parallel", "arbitrary")),
        name="global_gqa",
    )(q, kdup, vt)


def _swa_kernel(sink_ref, q_ref, kc_ref, vtc_ref, kp_ref, kq_ref, kn_ref, vtp_ref, vtq_ref, vtn_ref,
                o_ref, *, ctx_tiles, tile_off, n_tiles_all):
    tq = q_ref.shape[1]
    t = pl.program_id(1) + tile_off
    lane2 = lax.broadcasted_iota(jnp.int32, (1, 2 * tq), 1)
    sinks = [jnp.where(lane2 < tq, sink_ref[2 * j] * LOG2E, sink_ref[2 * j + 1] * LOG2E)
             for j in range(N_HEADS // 2)]
    kc_rows = vtc_ref.shape[3]
    ctx_blocks = [(lambda g, j=j: kc_ref[0, j * kc_rows:(j + 1) * kc_rows, 128 * g:128 * g + 128],
                   lambda g, j=j: vtc_ref[0, j, HEAD_DIM * g:HEAD_DIM * (g + 1), :],
                   None) for j in range(vtc_ref.shape[1])]

    def near(k_ref, vt_ref, mask):
        return (lambda g: k_ref[0, :, 128 * g:128 * g + 128],
                lambda g: vt_ref[0, 0, HEAD_DIM * g:HEAD_DIM * (g + 1), :], mask)

    @pl.when(t < ctx_tiles)
    def _():
        _attend(q_ref, o_ref, ctx_blocks, sinks)

    @pl.when(t >= ctx_tiles)
    def _():
        key_i = lax.broadcasted_iota(jnp.int32, (tq, 2 * tq), 0)
        qry_i = lax.broadcasted_iota(jnp.int32, (tq, 2 * tq), 1) & (tq - 1)
        lim_prev = jnp.where(t > ctx_tiles, 0, tq)
        lim_next = jnp.where(t < n_tiles_all - 1, 0, tq)
        _attend(q_ref, o_ref,
                ctx_blocks + [near(kp_ref, vtp_ref, key_i - qry_i >= lim_prev),
                              near(kq_ref, vtq_ref, None),
                              near(kn_ref, vtn_ref, qry_i - key_i >= lim_next)], sinks)


def _swa_call(sink, q, kdup, vt, ctx_len, lat_only):
    bsz, t_all, _ = q.shape
    _, _, vw, kc_rows = vt.shape
    tq = ATT_TQ
    per = kc_rows // tq
    ctx_tiles = ctx_len // tq
    n_all = t_all // tq
    off = ctx_tiles if lat_only else 0
    cur = lambda b, t: t + off
    prv = lambda b, t: jnp.maximum(t + off - 1, ctx_tiles)
    nxt = lambda b, t: jnp.minimum(t + off + 1, n_all - 1)
    k_spec = lambda tile: pl.BlockSpec((1, tq, KV_DUP_W), lambda b, t: (b, tile(b, t), 0))
    vt_spec = lambda tile: pl.BlockSpec((1, 1, vw, tq),
                                        lambda b, t: (b, tile(b, t) // per, 0, tile(b, t) % per))
    return pl.pallas_call(
        functools.partial(_swa_kernel, ctx_tiles=ctx_tiles, tile_off=off, n_tiles_all=n_all),
        grid=(bsz, n_all - off),
        in_specs=[pl.BlockSpec(memory_space=pltpu.SMEM),
                  pl.BlockSpec((1, tq, MIX_W), lambda b, t: (b, t + off, 0)),
                  pl.BlockSpec((1, ctx_len, KV_DUP_W), lambda b, t: (b, 0, 0)),
                  pl.BlockSpec((1, ctx_len // kc_rows, vw, kc_rows), lambda b, t: (b, 0, 0, 0)),
                  k_spec(prv), k_spec(cur), k_spec(nxt), vt_spec(prv), vt_spec(cur), vt_spec(nxt)],
        out_specs=pl.BlockSpec((1, tq, MIX_W), lambda b, t: (b, t, 0)),
        out_shape=jax.ShapeDtypeStruct((bsz, t_all - off * tq, MIX_W), BF16),
        compiler_params=_cparams(("parallel", "arbitrary")),
        name="window_gqa",
    )(sink, q, kdup, vt, kdup, kdup, kdup, vt, vt, vt)


def _segment_flags(t, ctx_tiles, n_tiles):
    first = (t == 0) | (t == ctx_tiles)
    last = (t == ctx_tiles - 1) | (t == n_tiles - 1)
    return first, last


def _rwkv_prep_kernel(f_ref, fp_ref, fn_ref, mup_ref, mun_ref, w0_ref, w2_ref, a0_ref, a2_ref, g2_ref,
                      kk_w_ref, ka_ref, rk_ref, seg_ref,
                      r_ref, v_ref, kk_ref, lw_ref, bb_ref, kd_ref, bonus_ref, g_ref,
                      *, ctx_tiles, n_tiles):
    first, last = _segment_flags(pl.program_id(1), ctx_tiles, n_tiles)
    cur = f_ref[0]
    prev8 = jnp.where(first, 0.0, fp_ref[0])
    next8 = jnp.where(last, 0.0, fn_ref[0])
    prv = _shift_rows(cur, prev8, next8, 1)
    nxt = _shift_rows(cur, prev8, next8, -1)
    f = cur + mup_ref[...] * (prv - cur) + mun_ref[...] * (nxt - cur)
    r = f[:, 0:MIX_W]
    k = f[:, MIX_W:2 * MIX_W]
    v = f[:, 2 * MIX_W:3 * MIX_W]
    wd = f[:, 3 * MIX_W:3 * MIX_W + B_LORA_W]
    ad = f[:, 3 * MIX_W + B_LORA_W:3 * MIX_W + 2 * B_LORA_W]
    gd = f[:, 3 * MIX_W + 2 * B_LORA_W:]
    wlog = w0_ref[...] + _dot(jnp.tanh(wd), w2_ref[...])
    lw_ref[0] = -math.exp(-0.5) * _sigmoid(wlog)
    a = _sigmoid(a0_ref[...] + _dot(ad, a2_ref[...]))
    kk = k * kk_w_ref[...]
    kk = kk * lax.rsqrt(_dot(kk * kk, seg_ref[...]) + 1e-12)
    ksum = jnp.zeros_like(k)
    for d in range(2):
        a_d = a[:, d * MIX_W:(d + 1) * MIX_W]
        k_d = k * (1.0 + (a_d - 1.0) * ka_ref[...])
        bb_ref[0, :, d * MIX_W:(d + 1) * MIX_W] = (a_d * kk).astype(bb_ref.dtype)
        kd_ref[0, :, d * MIX_W:(d + 1) * MIX_W] = k_d.astype(kd_ref.dtype)
        ksum = ksum + k_d
    r_ref[0] = r.astype(r_ref.dtype)
    v_ref[0] = v.astype(v_ref.dtype)
    kk_ref[0] = kk.astype(kk_ref.dtype)
    bonus_ref[0] = (_dot_exact_rhs(r * ksum * rk_ref[...], seg_ref[...]) * v).astype(bonus_ref.dtype)
    g_ref[0] = _dot(_sigmoid(gd), g2_ref[...]).astype(g_ref.dtype)


def _rwkv_prep_call(f, p, ctx_len, tm):
    bsz, t_all, _ = f.shape
    nt = t_all // tm
    ctx_tiles = ctx_len // tm
    nb8 = t_all // HALO
    const = lambda b, t: (0, 0)
    tok = lambda b, t: (b, t, 0)
    wide = jax.ShapeDtypeStruct((bsz, t_all, 2 * MIX_W), BF16)
    one = jax.ShapeDtypeStruct((bsz, t_all, MIX_W), BF16)
    logs = jax.ShapeDtypeStruct((bsz, t_all, 2 * MIX_W), F32)
    row = lambda w: pl.BlockSpec((1, w), const)
    return pl.pallas_call(
        functools.partial(_rwkv_prep_kernel, ctx_tiles=ctx_tiles, n_tiles=nt),
        grid=(bsz, nt),
        in_specs=[pl.BlockSpec((1, tm, B_IN), tok),
                  pl.BlockSpec((1, HALO, B_IN), lambda b, t: (b, jnp.maximum(t * (tm // HALO) - 1, 0), 0)),
                  pl.BlockSpec((1, HALO, B_IN), lambda b, t: (b, jnp.minimum((t + 1) * (tm // HALO), nb8 - 1), 0)),
                  row(B_IN), row(B_IN),
                  row(2 * MIX_W), pl.BlockSpec((B_LORA_W, 2 * MIX_W), const),
                  row(2 * MIX_W), pl.BlockSpec((B_LORA_W, 2 * MIX_W), const),
                  pl.BlockSpec((B_LORA_W, MIX_W), const),
                  row(MIX_W), row(MIX_W), row(MIX_W),
                  pl.BlockSpec((MIX_W, MIX_W), const)],
        out_specs=[pl.BlockSpec((1, tm, MIX_W), tok)] * 3 + [pl.BlockSpec((1, tm, 2 * MIX_W), tok)] * 3
                  + [pl.BlockSpec((1, tm, MIX_W), tok)] * 2,
        out_shape=[one, one, one, logs, wide, wide, one, one],
        compiler_params=_cparams(("parallel", "arbitrary")),
        name="rwkv_prep",
    )(f, f, f, p["mu_prev"], p["mu_next"], p["w0"], p["w2"], p["a0"], p["a2"], p["g2"],
      p["k_k"], p["k_a"], p["r_k"], p["seg"])


def _scan_chunk(s, d, n_ctx, n_all):
    bwd = jnp.where(s < n_ctx, n_ctx - 1 - s, n_all - 1 + n_ctx - s)
    return jnp.where(d == 0, s, bwd)


def _block_diag2(x):
    lane = lax.broadcasted_iota(jnp.int32, x.shape, 1)
    zero = jnp.zeros_like(x)
    return jnp.concatenate([jnp.where(lane < HEAD_DIM, x, zero), jnp.where(lane < HEAD_DIM, zero, x)], axis=0)


def _rwkv_scan_kernel(rf_ref, vf_ref, kkf_ref, lwf_ref, bbf_ref, kdf_ref,
                      rb_ref, vb_ref, kkb_ref, lwb_ref, bbb_ref, kdb_ref,
                      yf_ref, yb_ref, state_ref):
    c = rf_ref.shape[1]

    @pl.when(pl.program_id(1) == 0)
    def _():
        state_ref[...] = jnp.zeros_like(state_ref)

    ri = lax.broadcasted_iota(jnp.int32, (c, c), 0)
    ci = lax.broadcasted_iota(jnp.int32, (c, c), 1)
    ri2 = lax.broadcasted_iota(jnp.int32, (c, 2 * c), 0)
    ci2 = lax.broadcasted_iota(jnp.int32, (c, 2 * c), 1) & (c - 1)
    eye2 = jnp.where(ri2 == ci2, 1.0, 0.0)

    units = []
    dir_refs = ((rf_ref, vf_ref, kkf_ref, lwf_ref, bbf_ref, kdf_ref),
                (rb_ref, vb_ref, kkb_ref, lwb_ref, bbb_ref, kdb_ref))
    for d, bi in [(d, bi) for d in range(2) for bi in range(rf_ref.shape[0])]:
        r_ref, v_ref, kk_ref, lw_ref, bb_ref, kd_ref = dir_refs[d]
        incl = (ri - ci if d == 0 else ci - ri) >= 0
        lw = lw_ref[bi]
        cum = _dot_exact_lhs(jnp.where(incl, 1.0, 0.0).astype(BF16), lw)
        e_cum = jnp.exp(cum)
        e_inv = jnp.exp(-cum)
        e_tot = jnp.exp(jnp.sum(lw, axis=0, keepdims=True))
        qt = r_ref[bi] * e_cum
        kkt = kk_ref[bi] * jnp.exp(cum - lw)
        kh = kd_ref[bi] * e_inv
        bh = bb_ref[bi] * e_inv
        k_end = kh * e_tot
        b_end = -(bh * e_tot)
        v = v_ref[bi].astype(F32)
        incl2 = (ri2 - ci2 if d == 0 else ci2 - ri2) >= 0
        strict2 = (ri2 - ci2 if d == 0 else ci2 - ri2) > 0
        for j in range(N_HEADS // 2):
            sl = slice(2 * HEAD_DIM * j, 2 * HEAD_DIM * (j + 1))
            units.append(dict(
                d=d, bi=bi, j=j, sl=sl, incl=incl2, strict=strict2, v=v[:, sl], e_tot=e_tot[:, sl],
                lhs=jnp.concatenate([qt[:, sl], kkt[:, sl]], axis=0).astype(BF16),
                rhs=jnp.concatenate([_block_diag2(kh[:, sl].astype(BF16)),
                                     _block_diag2(bh[:, sl].astype(BF16))], axis=0),
                x_end=jnp.concatenate([k_end[:, sl], b_end[:, sl]], axis=0).astype(BF16)))

    c2 = 2 * c
    for u in units:
        a = _dot_nt(u["lhs"], u["rhs"])
        u["a_v"] = jnp.concatenate([jnp.where(u["incl"], a[:c, :c2], 0.0),
                                    jnp.where(u["strict"], a[c:, :c2], 0.0)], axis=0).astype(BF16)
        u["a_rb"] = jnp.where(u["incl"], a[:c, c2:], 0.0).astype(BF16)
        u["p"] = -jnp.where(u["strict"], a[c:, c2:], 0.0)
    for u in units:
        pb = u["p"].astype(BF16)
        u["t"] = eye2 + u["p"]
        u["p"] = jnp.dot(pb, _block_diag2(pb), preferred_element_type=F32)
    m = 2
    while 2 * m < c:
        for u in units:
            pb = u["p"].astype(BF16)
            both = jnp.dot(jnp.concatenate([u["t"].astype(BF16), pb], axis=0), _block_diag2(pb),
                           preferred_element_type=F32)
            u["t"] = u["t"] + both[:c]
            u["p"] = both[c:]
        m *= 2
    for u in units:
        u["t"] = (u["t"] + _dot(u["t"], _block_diag2(u["p"].astype(BF16)))).astype(BF16)
        u["av"] = _dot(u["a_v"], _block_diag2(u["v"].astype(BF16)))
    for u in units:
        u["st"] = state_ref[u["d"], u["bi"], u["j"]]
        u["x1"] = _dot_nt(u["lhs"], u["st"]) + u["av"]
    for u in units:
        u["u"] = _dot(u["t"], _block_diag2(u["x1"][c:].astype(BF16)))
    rr = lax.broadcasted_iota(jnp.int32, (c2, c2), 0)
    cc = lax.broadcasted_iota(jnp.int32, (c2, c2), 1)
    same_head = (rr // HEAD_DIM) == (cc // HEAD_DIM)
    for u in units:
        y = u["x1"][:c] - _dot(u["a_rb"], _block_diag2(u["u"].astype(BF16)))
        (yf_ref, yb_ref)[u["d"]][u["bi"], :, u["sl"]] = y
        wv = jnp.concatenate([u["v"], u["u"]], axis=0)
        upd = jnp.where(same_head, _dot(wv.T, u["x_end"]), 0.0)
        state_ref[u["d"], u["bi"], u["j"]] = u["st"] * u["e_tot"] + upd


def _rwkv_scan_call(r, v, kk, lw, bb, kd, ctx_len):
    bsz, t_all, _ = r.shape
    c = RWKV_CHUNK
    nb = RWKV_SAMPLES_PER_STEP if bsz % RWKV_SAMPLES_PER_STEP == 0 else 1
    n_all = t_all // c
    n_ctx = ctx_len // c
    blk = lambda im: pl.BlockSpec((nb, c, MIX_W), im)
    fwd = lambda b, s: (b, s, 0)
    bwd = lambda b, s: (b, _scan_chunk(s, 1, n_ctx, n_all), 0)
    bwd1 = lambda b, s: (b, _scan_chunk(s, 1, n_ctx, n_all), 1)
    out = jax.ShapeDtypeStruct((bsz, t_all, MIX_W), F32)
    return pl.pallas_call(
        _rwkv_scan_kernel,
        grid=(bsz // nb, n_all),
        in_specs=[blk(fwd)] * 6 + [blk(bwd)] * 3 + [blk(bwd1)] * 3,
        out_specs=[blk(fwd), blk(bwd)],
        out_shape=[out, out],
        scratch_shapes=[pltpu.VMEM((2, nb, N_HEADS // 2, 2 * HEAD_DIM, 2 * HEAD_DIM), F32)],
        compiler_params=_cparams(("parallel", "arbitrary")),
        name="rwkv_scan",
    )(r, v, kk, lw, bb, kd, r, v, kk, lw, bb, kd)


def _conv_kernel(x_ref, xp_ref, xn_ref, w_ref, b_ref, o_ref, *, ctx_tiles, n_tiles):
    first, last = _segment_flags(pl.program_id(1), ctx_tiles, n_tiles)
    cur = x_ref[0].astype(F32)
    prev8 = jnp.where(first, 0.0, xp_ref[0].astype(F32)[HALO_BF16 - HALO:])
    next8 = jnp.where(last, 0.0, xn_ref[0].astype(F32)[:HALO])
    acc = b_ref[...] + w_ref[D_CONV // 2:D_CONV // 2 + 1, :] * cur
    for kpos in range(D_CONV):
        s = D_CONV // 2 - kpos
        if s != 0:
            acc = acc + w_ref[kpos:kpos + 1, :] * _shift_rows(cur, prev8, next8, s)
    o_ref[0] = (acc * _sigmoid(acc)).astype(o_ref.dtype)


def _conv_call(xbc, conv_w, conv_b, ctx_len, tm):
    bsz, t_all, w = xbc.shape
    nt = t_all // tm
    hb = HALO_BF16
    n_halo = t_all // hb
    return pl.pallas_call(
        functools.partial(_conv_kernel, ctx_tiles=ctx_len // tm, n_tiles=nt),
        grid=(bsz, nt),
        in_specs=[pl.BlockSpec((1, tm, w), lambda b, t: (b, t, 0)),
                  pl.BlockSpec((1, hb, w), lambda b, t: (b, jnp.maximum(t * (tm // hb) - 1, 0), 0)),
                  pl.BlockSpec((1, hb, w), lambda b, t: (b, jnp.minimum((t + 1) * (tm // hb), n_halo - 1), 0)),
                  pl.BlockSpec((D_CONV, w), lambda b, t: (0, 0)),
                  pl.BlockSpec((1, w), lambda b, t: (0, 0))],
        out_specs=pl.BlockSpec((1, tm, w), lambda b, t: (b, t, 0)),
        out_shape=jax.ShapeDtypeStruct((bsz, t_all, w), BF16),
        compiler_params=_cparams(("parallel", "arbitrary")),
        name="ssd_conv",
    )(xbc, xbc, xbc, conv_w, conv_b)


def _ssd_scan_kernel(uf_ref, dtf_ref, ub_ref, dtb_ref, bias_ref, a_ref, sel_ref, exp_ref,
                     yf_ref, yb_ref, state_ref):
    c = uf_ref.shape[1]

    @pl.when(pl.program_id(1) == 0)
    def _():
        state_ref[...] = jnp.zeros_like(state_ref)

    ri = lax.broadcasted_iota(jnp.int32, (c, c), 0)
    ci = lax.broadcasted_iota(jnp.int32, (c, c), 1)
    expand = exp_ref[...]
    gw = D_STATE
    gl = MIX_W // D_GROUPS
    hpg = N_HEADS // D_GROUPS
    dirs = []
    for d, bi in [(d, bi) for d in range(2) for bi in range(uf_ref.shape[0])]:
        u_ref, dt_ref = ((uf_ref, dtf_ref), (ub_ref, dtb_ref))[d]
        incl = (ri - ci if d == 0 else ci - ri) >= 0
        incl_m = jnp.where(incl, 1.0, 0.0).astype(BF16)
        dt_all = _softplus(dt_ref[bi] + bias_ref[...])
        dt_d = _dot_exact_rhs(dt_all, sel_ref[d])
        dta_d = _dot_exact_rhs(dt_all * a_ref[...], sel_ref[d])
        cum_d = _dot_exact_lhs(incl_m, dta_d)
        dta_x = _dot_exact_rhs(dta_d, expand)
        dirs.append(dict(d=d, bi=bi, u_ref=u_ref, incl=incl, cum_d=cum_d, cum_t=cum_d.T, dt_t=dt_d.T,
                         cum_x=_dot_exact_lhs(incl_m, dta_x), dt_x=_dot_exact_rhs(dt_d, expand),
                         tot_x=jnp.sum(dta_x, axis=0, keepdims=True), xs=u_ref[bi, :, 0:MIX_W], ys=[]))

    def b_mat(p, g):
        return p["u_ref"][p["bi"], :, MIX_W + gw * g:MIX_W + gw * (g + 1)]

    def c_mat(p, g):
        return p["u_ref"][p["bi"], :, MIX_W + gw * D_GROUPS + gw * g:MIX_W + gw * D_GROUPS + gw * (g + 1)]

    def head_matrix(p, h):
        seg = p["cum_d"][:, h:h + 1] - p["cum_t"][h:h + 1, :]
        lmat = jnp.exp(jnp.where(p["incl"], seg, -jnp.inf))
        return (p["cb"] * lmat * p["dt_t"][h:h + 1, :]).astype(BF16)

    for g in range(D_GROUPS):
        for p in dirs:
            p["cb"] = _dot_nt(c_mat(p, g), b_mat(p, g))
        for r in range(0, hpg, 2):
            h = g * hpg + r
            for p in dirs:
                mm = jnp.concatenate([head_matrix(p, h), head_matrix(p, h + 1)], axis=1)
                x_pair = p["xs"][:, HEAD_DIM * h:HEAD_DIM * (h + 2)]
                p["ys"].append(jnp.dot(mm, _block_diag2(x_pair), preferred_element_type=F32))

    for p in dirs:
        d, bi = p["d"], p["bi"]
        wj = jnp.exp(p["tot_x"] - p["cum_x"]) * p["dt_x"]
        xw = p["xs"] * wj
        e_tot = jnp.exp(p["tot_x"])
        offs = []
        for g in range(D_GROUPS):
            st = state_ref[d, bi, :, gl * g:gl * (g + 1)]
            offs.append(_dot(c_mat(p, g), st))
            bm_t = b_mat(p, g).astype(F32).T
            state_ref[d, bi, :, gl * g:gl * (g + 1)] = (st * e_tot[:, gl * g:gl * (g + 1)]
                                                        + _dot(bm_t, xw[:, gl * g:gl * (g + 1)]))
        (yf_ref, yb_ref)[d][bi] = (jnp.concatenate(p["ys"], axis=1)
                                   + jnp.concatenate(offs, axis=1) * jnp.exp(p["cum_x"]))


def _ssd_scan_call(u, dt_raw, dtb, a_row, sel, expand, ctx_len):
    bsz, t_all, w = u.shape
    c = SSD_CHUNK
    nb = SSD_SAMPLES_PER_STEP if bsz % SSD_SAMPLES_PER_STEP == 0 else 1
    n_all = t_all // c
    n_ctx = ctx_len // c
    fwd = lambda b, s: (b, s, 0)
    bwd = lambda b, s: (b, _scan_chunk(s, 1, n_ctx, n_all), 0)
    const = lambda b, s: (0, 0)
    out = jax.ShapeDtypeStruct((bsz, t_all, MIX_W), F32)
    return pl.pallas_call(
        _ssd_scan_kernel,
        grid=(bsz // nb, n_all),
        in_specs=[pl.BlockSpec((nb, c, w), fwd), pl.BlockSpec((nb, c, DT_PAD), fwd),
                  pl.BlockSpec((nb, c, w), bwd), pl.BlockSpec((nb, c, DT_PAD), bwd),
                  pl.BlockSpec((1, DT_PAD), const),
                  pl.BlockSpec((1, DT_PAD), const),
                  pl.BlockSpec((2, DT_PAD, DT_PAD), lambda b, s: (0, 0, 0)),
                  pl.BlockSpec((DT_PAD, MIX_W), const)],
        out_specs=[pl.BlockSpec((nb, c, MIX_W), fwd), pl.BlockSpec((nb, c, MIX_W), bwd)],
        out_shape=[out, out],
        scratch_shapes=[pltpu.VMEM((2, nb, D_STATE, MIX_W), F32)],
        compiler_params=_cparams(("parallel", "arbitrary")),
        name="ssd_scan",
    )(u, dt_raw, u, dt_raw, dtb, a_row, sel, expand)


def _out_proj(x_ref, mod_ref, att_ref, mix, wo_ref, o_ref):
    out = (jnp.dot(att_ref[0], wo_ref[0:MIX_W, :], preferred_element_type=F32)
           + jnp.dot(mix.astype(BF16), wo_ref[MIX_W:, :], preferred_element_type=F32))
    o_ref[0] = x_ref[0] + mod_ref[0, 2:3, :] * out


def _post_rwkv_kernel(x_ref, mod_ref, att_ref, yf_ref, yb_ref, bonus_ref, g_ref, lnw_ref, lnb_ref,
                      seg_ref, wo_ref, o_ref):
    y = yf_ref[0] + yb_ref[0]
    inv = 1.0 / HEAD_DIM
    mu = _dot_exact_rhs(y, seg_ref[...]) * inv
    yc = y - mu
    var = _dot(yc * yc, seg_ref[...]) * inv
    yn = yc * lax.rsqrt(var + B_GN_EPS) * lnw_ref[...] + lnb_ref[...]
    _out_proj(x_ref, mod_ref, att_ref, (yn + bonus_ref[0]) * g_ref[0], wo_ref, o_ref)


def _post_ssd_kernel(x_ref, mod_ref, att_ref, yf_ref, yb_ref, u_ref, z_ref, dsk_ref, nw_ref,
                     wo_ref, o_ref):
    z = z_ref[0].astype(F32)
    y = (yf_ref[0] + yb_ref[0] + dsk_ref[...] * u_ref[0].astype(F32)) * (z * _sigmoid(z))
    gl = MIX_W // D_GROUPS
    parts = []
    for g in range(D_GROUPS):
        yg = y[:, gl * g:gl * (g + 1)]
        parts.append(yg * lax.rsqrt(jnp.mean(yg * yg, axis=-1, keepdims=True) + NORM_EPS))
    _out_proj(x_ref, mod_ref, att_ref, jnp.concatenate(parts, axis=1) * nw_ref[...], wo_ref, o_ref)


def _post_call(kind, x_all, mod, att, extras, rows, w_out, n_batch, ctx_len, tm, lat_only):
    bsz, t_all, _ = x_all.shape
    ctx_tiles = ctx_len // tm
    off = ctx_tiles if lat_only else 0
    nt = t_all // tm - off
    tok = lambda b, t: (b, t + off, 0)
    att_off = 0 if att.shape[1] == t_all else -off
    const = lambda b, t: (0, 0)
    in_specs = [pl.BlockSpec((1, tm, D_MODEL), tok),
                _mod_spec(n_batch, ctx_tiles, off),
                pl.BlockSpec((1, tm, MIX_W), lambda b, t: (b, t + off + att_off, 0))]
    in_specs += [pl.BlockSpec((1, tm, MIX_W), tok) for _ in extras]
    in_specs += [pl.BlockSpec(r.shape, const) for r in rows]
    in_specs += [pl.BlockSpec((D_MODEL, D_MODEL), const)]
    body = _post_rwkv_kernel if kind == "rwkv" else _post_ssd_kernel
    return pl.pallas_call(
        body,
        grid=(bsz, nt),
        in_specs=in_specs,
        out_specs=pl.BlockSpec((1, tm, D_MODEL), lambda b, t: (b, t, 0)),
        out_shape=jax.ShapeDtypeStruct((bsz, nt * tm, D_MODEL), F32),
        compiler_params=_cparams(("parallel", "arbitrary")),
        name="post_" + kind,
    )(x_all, mod, att, *extras, *rows, w_out)


def _mlp_kernel(x_ref, mod_ref, nw_ref, w1_ref, w2_ref, fw_ref, o_ref, *, final):
    x = x_ref[0]
    ms = jnp.mean(x * x, axis=-1, keepdims=True)
    h = x * lax.rsqrt(ms + NORM_EPS) * nw_ref[...]
    hb = (h * (1.0 + mod_ref[0, 4:5, :]) + mod_ref[0, 3:4, :]).astype(BF16)
    acc = jnp.zeros_like(x)
    for j in range(MLP_HIDDEN // D_MODEL):
        a = jnp.maximum(jnp.dot(hb, w1_ref[:, D_MODEL * j:D_MODEL * (j + 1)], preferred_element_type=F32), 0.0)
        acc = acc + jnp.dot((a * a).astype(BF16), w2_ref[D_MODEL * j:D_MODEL * (j + 1), :],
                            preferred_element_type=F32)
    x = x + mod_ref[0, 5:6, :] * acc
    if final:
        x = x * lax.rsqrt(jnp.mean(x * x, axis=-1, keepdims=True) + NORM_EPS) * fw_ref[...]
    o_ref[0] = x


def _mlp_call(x_in, mod, norm_w, w1, w2, final_w, n_batch, ctx_tiles, tm, final):
    bsz, t_in, _ = x_in.shape
    const = lambda b, t: (0, 0)
    tok = lambda b, t: (b, t, 0)
    weights = lambda shape: pl.BlockSpec(shape, const, pipeline_mode=pl.Buffered(1))
    return pl.pallas_call(
        functools.partial(_mlp_kernel, final=final),
        grid=(bsz, t_in // tm),
        in_specs=[pl.BlockSpec((1, tm, D_MODEL), tok),
                  _mod_spec(n_batch, ctx_tiles, 0),
                  pl.BlockSpec((1, D_MODEL), const),
                  weights((D_MODEL, MLP_HIDDEN)),
                  weights((MLP_HIDDEN, D_MODEL)),
                  pl.BlockSpec((1, D_MODEL), const)],
        out_specs=pl.BlockSpec((1, tm, D_MODEL), tok),
        out_shape=jax.ShapeDtypeStruct((bsz, t_in, D_MODEL), F32),
        compiler_params=_cparams(("parallel", "arbitrary")),
        name="mlp_final" if final else "mlp",
    )(x_in, mod, norm_w, w1, w2, final_w)


def _rope_tables(n_lat, ctx_len):
    rows = n_lat // GRID_W
    row = jnp.repeat(jnp.arange(rows, dtype=F32), GRID_W)
    col = jnp.tile(jnp.arange(GRID_W, dtype=F32), rows)
    n_freq = HEAD_DIM // 4
    inv_freq = ROPE_THETA ** (-jnp.arange(n_freq, dtype=F32) / n_freq)
    ang_r = row[:, None] * inv_freq
    ang_c = col[:, None] * inv_freq
    cr, sr, cc, sc = jnp.cos(ang_r), jnp.sin(ang_r), jnp.cos(ang_c), jnp.sin(ang_c)
    cos = jnp.concatenate([cr, cr, cc, cc], axis=-1)
    sin = jnp.concatenate([-sr, sr, -sc, sc], axis=-1)
    cos = jnp.concatenate([jnp.ones((ctx_len, HEAD_DIM), F32), cos], axis=0)
    sin = jnp.concatenate([jnp.zeros((ctx_len, HEAD_DIM), F32), sin], axis=0)
    return jnp.tile(cos, (1, N_HEADS)), jnp.tile(sin, (1, N_HEADS))


def _dup_kv_cols(w):
    k0, k1 = w[:, :HEAD_DIM], w[:, HEAD_DIM:]
    return jnp.concatenate([k0, k0, k1, k1], axis=1)


def _seg_matrix():
    return jnp.asarray(np.kron(np.eye(N_HEADS), np.ones((HEAD_DIM, HEAD_DIM))), BF16)


def _tile_rows(ctx_len):
    return 256 if ctx_len % 256 == 0 else 128


def _forward(x, c, ctx, c_ctx, w_mod, b_mod, norm_mix, norm_mlp, w_out, mlp_w1, mlp_w2, final_norm,
             ab_w_in, a_q_norm, a_k_norm, b_mu_prev, b_mu_next, b_w0, b_w2, b_a0, b_a2, b_g2,
             b_k_k, b_k_a, b_r_k, b_ln_w, b_ln_b,
             cd_w_in, c_sink, d_conv_w, d_conv_b, d_dt_bias, d_A_log, d_D, d_norm_w):
    bsz, n_lat, _ = x.shape
    ctx_len = ctx.shape[1]
    tm = _tile_rows(ctx_len)
    ctx_tiles = ctx_len // tm
    mod_rows = -(-(bsz + 1) // 8) * 8
    c_all = jnp.concatenate([c, c_ctx[None, :], jnp.zeros((mod_rows - bsz - 1, D_MODEL), F32)], axis=0)
    mods = _mod_call(c_all, w_mod, b_mod).reshape(DEPTH, mod_rows, 6, D_MODEL)
    cos, sin = _rope_tables(n_lat, ctx_len)
    seg = _seg_matrix()
    row = lambda a: a.reshape(1, -1).astype(F32)
    x_all = jnp.concatenate([ctx, x], axis=1)
    head_dup = lambda w, n: jnp.tile(w.astype(F32), n).reshape(1, -1)

    for i in range(DEPTH):
        j = i // 2
        last = i == DEPTH - 1
        mod = mods[i]
        if i % 2 == 0:
            w = ab_w_in[j]
            w_in = jnp.concatenate([w[:, :MIX_W], _dup_kv_cols(w[:, MIX_W:MIX_W + 128]), w[:, MIX_W + 128:]],
                                   axis=1).astype(BF16)
            q, kdup, vt, f = _premix_call(
                x_all, mod, row(norm_mix[i]), w_in, cos, sin, head_dup(a_q_norm[j], N_HEADS),
                head_dup(a_k_norm[j], 2 * KV_HEADS), seg, (MIX_W, KV_DUP_W, KV_W, B_IN),
                (BF16, BF16, BF16, F32), True, bsz, ctx_tiles, tm)
            att = _gqa_call(q, kdup, vt, ctx_len)
            w2cat = jnp.zeros((B_LORA_W, 2 * MIX_W), F32)
            w2cat = w2cat.at[:64, :MIX_W].set(b_w2[j, 0]).at[64:, MIX_W:].set(b_w2[j, 1])
            a2cat = jnp.zeros((B_LORA_W, 2 * MIX_W), F32)
            a2cat = a2cat.at[:64, :MIX_W].set(b_a2[j, 0]).at[64:, MIX_W:].set(b_a2[j, 1])
            p = dict(mu_prev=row(b_mu_prev[j]), mu_next=row(b_mu_next[j]), w0=row(b_w0[j]),
                     w2=w2cat.astype(BF16), a0=row(b_a0[j]), a2=a2cat.astype(BF16),
                     g2=b_g2[j].astype(BF16), k_k=row(b_k_k[j]), k_a=row(b_k_a[j]), r_k=row(b_r_k[j]),
                     seg=seg)
            r, v, kk, lw, bb, kd, bonus, g = _rwkv_prep_call(f, p, ctx_len, tm)
            yf, yb = _rwkv_scan_call(r, v, kk, lw, bb, kd, ctx_len)
            x_all = _post_call("rwkv", x_all, mod, att, [yf, yb, bonus, g],
                               [row(b_ln_w[j]), row(b_ln_b[j]), seg], w_out[i].astype(BF16),
                               bsz, ctx_len, tm, False)
        else:
            w = cd_w_in[j]
            w_in = jnp.concatenate([w[:, :MIX_W], _dup_kv_cols(w[:, MIX_W:MIX_W + 128]), w[:, MIX_W + 128:],
                                    jnp.zeros((D_MODEL, DT_PAD - 2 * N_HEADS), F32)], axis=1).astype(BF16)
            q, kdup, vt, z, xbc, dt_raw = _premix_call(
                x_all, mod, row(norm_mix[i]), w_in, cos, sin, jnp.ones((1, MIX_W), F32),
                jnp.ones((1, KV_DUP_W), F32), seg, (MIX_W, KV_DUP_W, KV_W, MIX_W, D_XBC, DT_PAD),
                (BF16, BF16, BF16, BF16, BF16, F32), False, bsz, ctx_tiles, tm)
            att = _swa_call(c_sink[j].astype(F32), q, kdup, vt, ctx_len, last)
            u = _conv_call(xbc, d_conv_w[j], row(d_conv_b[j]), ctx_len, tm)
            pad = jnp.zeros((DT_PAD - 2 * N_HEADS,), F32)
            dtb = jnp.concatenate([d_dt_bias[j].reshape(-1), pad]).reshape(1, -1)
            a_row = jnp.concatenate([-jnp.exp(d_A_log[j].astype(F32)).reshape(-1), pad]).reshape(1, -1)
            sel = np.zeros((2, DT_PAD, DT_PAD), np.float32)
            expand = np.zeros((DT_PAD, MIX_W), np.float32)
            for h in range(N_HEADS):
                sel[0, h, h] = 1.0
                sel[1, N_HEADS + h, h] = 1.0
                expand[h, HEAD_DIM * h:HEAD_DIM * (h + 1)] = 1.0
            yf, yb = _ssd_scan_call(u, dt_raw, dtb, a_row, jnp.asarray(sel, BF16), jnp.asarray(expand, BF16), ctx_len)
            dsk = jnp.repeat(d_D[j].astype(F32), HEAD_DIM).reshape(1, -1)
            x_all = _post_call("ssd", x_all, mod, att, [yf, yb, u, z], [dsk, row(d_norm_w[j])],
                               w_out[i].astype(BF16), bsz, ctx_len, tm, last)
        x_all = _mlp_call(x_all, mod, row(norm_mlp[i]), mlp_w1[i].astype(BF16), mlp_w2[i].astype(BF16),
                          row(final_norm), bsz, 0 if last else ctx_tiles, tm, last)
    return x_all


def kernel(x, c, ctx, c_ctx, w_mod, b_mod, norm_mix, norm_mlp, w_out, mlp_w1, mlp_w2, final_norm, ab_w_in, a_q_norm, a_k_norm, b_mu_prev, b_mu_next, b_w0, b_w2, b_a0, b_a2, b_g2, b_k_k, b_k_a, b_r_k, b_ln_w, b_ln_b, cd_w_in, c_sink, d_conv_w, d_conv_b, d_dt_bias, d_A_log, d_D, d_norm_w):
    return _forward(x, c, ctx, c_ctx, w_mod, b_mod, norm_mix, norm_mlp, w_out, mlp_w1, mlp_w2, final_norm,
                    ab_w_in, a_q_norm, a_k_norm, b_mu_prev, b_mu_next, b_w0, b_w2, b_a0, b_a2, b_g2,
                    b_k_k, b_k_a, b_r_k, b_ln_w, b_ln_b,
                    cd_w_in, c_sink, d_conv_w, d_conv_b, d_dt_bias, d_A_log, d_D, d_norm_w)
```

```python
import functools
import math

import numpy as np
import jax
import jax.numpy as jnp
from jax import lax
from jax.experimental import pallas as pl
from jax.experimental.pallas import tpu as pltpu

F32 = jnp.float32
BF16 = jnp.bfloat16

D_MODEL = 1024
DEPTH = 4
GRID_W = 64
HEAD_DIM = 64
ROPE_THETA = 10000.0
NORM_EPS = 1e-6
MLP_HIDDEN = 4 * D_MODEL
N_HEADS = 8
MIX_W = N_HEADS * HEAD_DIM
KV_HEADS = 2
KV_W = KV_HEADS * HEAD_DIM
KV_DUP_W = 2 * KV_W
B_LORA_W = 128
B_IN = 3 * MIX_W + 3 * B_LORA_W
B_GN_EPS = 64e-5
WINDOW = 128
D_STATE = 128
D_GROUPS = 2
D_CONV = 5
D_XBC = MIX_W + 2 * D_GROUPS * D_STATE
DT_PAD = 128
SSD_CHUNK = 128
SSD_SAMPLES_PER_STEP = 4
RWKV_CHUNK = 64
RWKV_SAMPLES_PER_STEP = 4
ATT_TQ = 128
LOG2E = math.log2(math.e)
Q_SCALE = HEAD_DIM ** -0.5 * LOG2E
ONES_ROWS = 16
HALO = 8
HALO_BF16 = 16
VMEM_LIMIT = 56 * 1024 * 1024


def _cparams(sem):
    return pltpu.CompilerParams(dimension_semantics=sem, vmem_limit_bytes=VMEM_LIMIT)


def _dot(a, b):
    return jnp.dot(a.astype(BF16), b.astype(BF16), preferred_element_type=F32)


def _dot_nt(a, b):
    return lax.dot_general(a.astype(BF16), b.astype(BF16), (((1,), (1,)), ((), ())),
                           preferred_element_type=F32)


def _split_hi_lo(x):
    hi = x.astype(BF16)
    lo = (x - hi.astype(F32)).astype(BF16)
    return hi, lo


def _dot_exact_rhs(x, m):
    hi, lo = _split_hi_lo(x)
    return (jnp.dot(hi, m, preferred_element_type=F32)
            + jnp.dot(lo, m, preferred_element_type=F32))


def _dot_exact_lhs(m, x):
    hi, lo = _split_hi_lo(x)
    return (jnp.dot(m, hi, preferred_element_type=F32)
            + jnp.dot(m, lo, preferred_element_type=F32))


def _rope(x, cos, sin_signed):
    w = x.shape[-1]
    lane = lax.broadcasted_iota(jnp.int32, x.shape, 1)
    partner = jnp.where((lane & 16) == 0, pltpu.roll(x, w - 16, 1), pltpu.roll(x, 16, 1))
    return x * cos + partner * sin_signed


def _shift_rows(cur, prev8, next8, s):
    n = cur.shape[0]
    if s == 0:
        return cur
    ri = lax.broadcasted_iota(jnp.int32, (HALO, cur.shape[1]), 0)
    if s > 0:
        rolled = pltpu.roll(cur, s, 0)
        fill = pltpu.roll(prev8, s, 0)
        head = jnp.where(ri < s, fill, rolled[:HALO])
        return jnp.concatenate([head, rolled[HALO:]], axis=0)
    a = -s
    rolled = pltpu.roll(cur, n - a, 0)
    fill = pltpu.roll(next8, HALO - a, 0)
    tail = jnp.where(ri >= HALO - a, fill, rolled[n - HALO:])
    return jnp.concatenate([rolled[:n - HALO], tail], axis=0)


def _sigmoid(x):
    return 1.0 / (1.0 + jnp.exp(-x))


def _softplus(x):
    return jnp.maximum(x, 0.0) + jnp.log(1.0 + jnp.exp(-jnp.abs(x)))


def _mod_kernel(c_ref, w_ref, b_ref, o_ref):
    c = c_ref[...]
    o_ref[0] = _dot(c * _sigmoid(c), w_ref[0]) + b_ref[0]


def _mod_call(c_all, w_mod, b_mod):
    rows = c_all.shape[0]
    n_col = w_mod.shape[2] // D_MODEL
    return pl.pallas_call(
        _mod_kernel,
        grid=(DEPTH, n_col),
        in_specs=[pl.BlockSpec((rows, D_MODEL), lambda i, j: (0, 0)),
                  pl.BlockSpec((1, D_MODEL, D_MODEL), lambda i, j: (i, 0, j)),
                  pl.BlockSpec((1, 1, D_MODEL), lambda i, j: (i, 0, j))],
        out_specs=pl.BlockSpec((1, rows, D_MODEL), lambda i, j: (i, 0, j)),
        out_shape=jax.ShapeDtypeStruct((DEPTH, rows, w_mod.shape[2]), F32),
        compiler_params=_cparams(("arbitrary", "arbitrary")),
        name="modulation",
    )(c_all, w_mod, b_mod.reshape(DEPTH, 1, -1))


def _mod_spec(n_batch, ctx_tiles, tile_off):
    return pl.BlockSpec((1, 6, D_MODEL),
                        lambda b, t: (jnp.where(t + tile_off < ctx_tiles, n_batch, b), 0, 0))


def _head_rms(y, seg):
    w = y.shape[-1]
    ssum = _dot(y * y, seg[:w, :w])
    return y * lax.rsqrt(ssum * (1.0 / HEAD_DIM) + NORM_EPS)


def _premix_kernel(x_ref, mod_ref, nw_ref, w_ref, cos_ref, sin_ref, qnw_ref, knw_ref, seg_ref,
                   *out_refs, widths, qk_norm):
    x = x_ref[0]
    ms = jnp.mean(x * x, axis=-1, keepdims=True)
    h = x * lax.rsqrt(ms + NORM_EPS) * nw_ref[...]
    h = h * (1.0 + mod_ref[0, 1:2, :]) + mod_ref[0, 0:1, :]
    hb = h.astype(BF16)
    off = 0
    for idx, (o_ref, wdt) in enumerate(zip(out_refs, widths)):
        y = jnp.dot(hb, w_ref[:, off:off + wdt], preferred_element_type=F32)
        if idx < 2:
            if qk_norm:
                nw = qnw_ref if idx == 0 else knw_ref
                y = _head_rms(y, seg_ref[...]) * nw[...]
            y = _rope(y, cos_ref[:, :wdt], sin_ref[:, :wdt])
            if idx == 0:
                y = y * Q_SCALE
        if idx == 2:
            o_ref[0, 0] = y.T.astype(o_ref.dtype)
        else:
            o_ref[0] = y.astype(o_ref.dtype)
        off += wdt


def _premix_call(x_all, mod, norm_w, w_in, cos, sin, qnw, knw, seg, widths, dtypes, qk_norm,
                 n_batch, ctx_tiles, tm):
    bsz, t_all, _ = x_all.shape
    nt = t_all // tm
    n_in = w_in.shape[1]
    const = lambda b, t: (0, 0)
    tok = lambda b, t: (b, t, 0)
    out_specs = [pl.BlockSpec((1, tm, w), tok) for w in widths]
    out_shape = [jax.ShapeDtypeStruct((bsz, t_all, w), d) for w, d in zip(widths, dtypes)]
    out_specs[2] = pl.BlockSpec((1, 1, widths[2], tm), lambda b, t: (b, t, 0, 0))
    out_shape[2] = jax.ShapeDtypeStruct((bsz, nt, widths[2], tm), dtypes[2])
    return pl.pallas_call(
        functools.partial(_premix_kernel, widths=widths, qk_norm=qk_norm),
        grid=(bsz, nt),
        in_specs=[pl.BlockSpec((1, tm, D_MODEL), tok),
                  _mod_spec(n_batch, ctx_tiles, 0),
                  pl.BlockSpec((1, D_MODEL), const),
                  pl.BlockSpec((D_MODEL, n_in), const),
                  pl.BlockSpec((tm, MIX_W), lambda b, t: (t, 0)),
                  pl.BlockSpec((tm, MIX_W), lambda b, t: (t, 0)),
                  pl.BlockSpec((1, MIX_W), const),
                  pl.BlockSpec((1, KV_DUP_W), const),
                  pl.BlockSpec((MIX_W, MIX_W), const)],
        out_specs=out_specs,
        out_shape=out_shape,
        compiler_params=_cparams(("parallel", "arbitrary")),
        name="premix_qknorm" if qk_norm else "premix",
    )(x_all, mod, norm_w, w_in, cos, sin, qnw, knw, seg)


NEG_BIG = -1e30
SCORE_LOOKAHEAD = 2


def _masked_q_pairs(q_ref):
    tq = q_ref.shape[1]
    lane = lax.broadcasted_iota(jnp.int32, (tq, 2 * HEAD_DIM), 1)
    pairs = []
    for j in range(N_HEADS // 2):
        slab = q_ref[0, :, 128 * j:128 * j + 128]
        zero = jnp.zeros_like(slab)
        pairs.append(jnp.concatenate([jnp.where(lane < HEAD_DIM, slab, zero),
                                      jnp.where(lane >= HEAD_DIM, slab, zero)], axis=0))
    return pairs


def _fold_rows_max(s):
    m = s[0:8]
    for i in range(1, s.shape[0] // 8):
        m = jnp.maximum(m, s[8 * i:8 * i + 8])
    return m


def _attend(q_ref, o_ref, blocks, sinks):
    tq = q_ref.shape[1]
    n_pairs = N_HEADS // 2
    ppg = n_pairs // KV_HEADS
    q_pairs = _masked_q_pairs(q_ref)
    row = lax.broadcasted_iota(jnp.int32, (HEAD_DIM + ONES_ROWS, 2 * tq), 0)
    if sinks is None:
        m = [jnp.full((1, 2 * tq), NEG_BIG, F32) for _ in range(n_pairs)]
        acc = [jnp.zeros((HEAD_DIM + ONES_ROWS, 2 * tq), F32) for _ in range(n_pairs)]
    else:
        m = list(sinks)
        acc = [jnp.where(row >= HEAD_DIM, 1.0, 0.0) for _ in range(n_pairs)]
    def scores(blk):
        k_of, _, mask = blk
        out = []
        for g in range(KV_HEADS):
            kc = k_of(g)
            for r in range(ppg):
                sc = _dot_nt(kc, q_pairs[g * ppg + r])
                out.append(sc if mask is None else jnp.where(mask, sc, NEG_BIG))
        return out

    ahead = [scores(b) for b in blocks[:SCORE_LOOKAHEAD]]
    for i, (_, vt_of, _) in enumerate(blocks):
        s = ahead.pop(0)
        if i + SCORE_LOOKAHEAD < len(blocks):
            ahead.append(scores(blocks[i + SCORE_LOOKAHEAD]))
        vts = []
        for g in range(KV_HEADS):
            vt = vt_of(g)
            vts.append(jnp.concatenate([vt, jnp.ones((ONES_ROWS, vt.shape[1]), BF16)], axis=0))
        p, alpha = [], []
        for j in range(n_pairs):
            m_new = jnp.maximum(m[j], jnp.max(_fold_rows_max(s[j]), axis=0, keepdims=True))
            alpha.append(jnp.exp2(m[j] - m_new))
            p.append(jnp.exp2(s[j] - m_new).astype(BF16))
            m[j] = m_new
        for j in range(n_pairs):
            acc[j] = alpha[j] * acc[j] + jnp.dot(vts[j // ppg], p[j], preferred_element_type=F32)
    outs = []
    for a in acc:
        o = a[:HEAD_DIM] / a[HEAD_DIM:HEAD_DIM + 1]
        outs += [o[:, :tq], o[:, tq:]]
    o_ref[0] = jnp.concatenate(outs, axis=0).T.astype(o_ref.dtype)


def _gqa_kernel(q_ref, k_ref, vt_ref, o_ref, *, ctx_chunks, ctx_tiles):
    kc_rows = vt_ref.shape[3]

    def blocks(n_chunks):
        return [(lambda g, j=j: k_ref[0, j * kc_rows:(j + 1) * kc_rows, 128 * g:128 * g + 128],
                 lambda g, j=j: vt_ref[0, j, HEAD_DIM * g:HEAD_DIM * (g + 1), :],
                 None) for j in range(n_chunks)]

    t = pl.program_id(1)

    @pl.when(t < ctx_tiles)
    def _():
        _attend(q_ref, o_ref, blocks(ctx_chunks), None)

    @pl.when(t >= ctx_tiles)
    def _():
        _attend(q_ref, o_ref, blocks(vt_ref.shape[1]), None)


def _gqa_call(q, kdup, vt, ctx_len):
    bsz, t_all, _ = q.shape
    _, n_chunks, vw, kc_rows = vt.shape
    tq = ATT_TQ
    return pl.pallas_call(
        functools.partial(_gqa_kernel, ctx_chunks=ctx_len // kc_rows, ctx_tiles=ctx_len // tq),
        grid=(bsz, t_all // tq),
        in_specs=[pl.BlockSpec((1, tq, MIX_W), lambda b, t: (b, t, 0)),
                  pl.BlockSpec((1, t_all, KV_DUP_W), lambda b, t: (b, 0, 0)),
                  pl.BlockSpec((1, n_chunks, vw, kc_rows), lambda b, t: (b, 0, 0, 0))],
        out_specs=pl.BlockSpec((1, tq, MIX_W), lambda b, t: (b, t, 0)),
        out_shape=jax.ShapeDtypeStruct((bsz, t_all, MIX_W), BF16),
        compiler_params=_cparams(("parallel", "arbitrary")),
        name="global_gqa",
    )(q, kdup, vt)


def _swa_kernel(sink_ref, q_ref, kc_ref, vtc_ref, kp_ref, kq_ref, kn_ref, vtp_ref, vtq_ref, vtn_ref,
                o_ref, *, ctx_tiles, tile_off, n_tiles_all):
    tq = q_ref.shape[1]
    t = pl.program_id(1) + tile_off
    lane2 = lax.broadcasted_iota(jnp.int32, (1, 2 * tq), 1)
    sinks = [jnp.where(lane2 < tq, sink_ref[2 * j] * LOG2E, sink_ref[2 * j + 1] * LOG2E)
             for j in range(N_HEADS // 2)]
    kc_rows = vtc_ref.shape[3]
    ctx_blocks = [(lambda g, j=j: kc_ref[0, j * kc_rows:(j + 1) * kc_rows, 128 * g:128 * g + 128],
                   lambda g, j=j: vtc_ref[0, j, HEAD_DIM * g:HEAD_DIM * (g + 1), :],
                   None) for j in range(vtc_ref.shape[1])]

    def near(k_ref, vt_ref, mask):
        return (lambda g: k_ref[0, :, 128 * g:128 * g + 128],
                lambda g: vt_ref[0, 0, HEAD_DIM * g:HEAD_DIM * (g + 1), :], mask)

    @pl.when(t < ctx_tiles)
    def _():
        _attend(q_ref, o_ref, ctx_blocks, sinks)

    @pl.when(t >= ctx_tiles)
    def _():
        key_i = lax.broadcasted_iota(jnp.int32, (tq, 2 * tq), 0)
        qry_i = lax.broadcasted_iota(jnp.int32, (tq, 2 * tq), 1) & (tq - 1)
        lim_prev = jnp.where(t > ctx_tiles, 0, tq)
        lim_next = jnp.where(t < n_tiles_all - 1, 0, tq)
        _attend(q_ref, o_ref,
                ctx_blocks + [near(kp_ref, vtp_ref, key_i - qry_i >= lim_prev),
                              near(kq_ref, vtq_ref, None),
                              near(kn_ref, vtn_ref, qry_i - key_i >= lim_next)], sinks)


def _swa_call(sink, q, kdup, vt, ctx_len, lat_only):
    bsz, t_all, _ = q.shape
    _, _, vw, kc_rows = vt.shape
    tq = ATT_TQ
    per = kc_rows // tq
    ctx_tiles = ctx_len // tq
    n_all = t_all // tq
    off = ctx_tiles if lat_only else 0
    cur = lambda b, t: t + off
    prv = lambda b, t: jnp.maximum(t + off - 1, ctx_tiles)
    nxt = lambda b, t: jnp.minimum(t + off + 1, n_all - 1)
    k_spec = lambda tile: pl.BlockSpec((1, tq, KV_DUP_W), lambda b, t: (b, tile(b, t), 0))
    vt_spec = lambda tile: pl.BlockSpec((1, 1, vw, tq),
                                        lambda b, t: (b, tile(b, t) // per, 0, tile(b, t) % per))
    return pl.pallas_call(
        functools.partial(_swa_kernel, ctx_tiles=ctx_tiles, tile_off=off, n_tiles_all=n_all),
        grid=(bsz, n_all - off),
        in_specs=[pl.BlockSpec(memory_space=pltpu.SMEM),
                  pl.BlockSpec((1, tq, MIX_W), lambda b, t: (b, t + off, 0)),
                  pl.BlockSpec((1, ctx_len, KV_DUP_W), lambda b, t: (b, 0, 0)),
                  pl.BlockSpec((1, ctx_len // kc_rows, vw, kc_rows), lambda b, t: (b, 0, 0, 0)),
                  k_spec(prv), k_spec(cur), k_spec(nxt), vt_spec(prv), vt_spec(cur), vt_spec(nxt)],
        out_specs=pl.BlockSpec((1, tq, MIX_W), lambda b, t: (b, t, 0)),
        out_shape=jax.ShapeDtypeStruct((bsz, t_all - off * tq, MIX_W), BF16),
        compiler_params=_cparams(("parallel", "arbitrary")),
        name="window_gqa",
    )(sink, q, kdup, vt, kdup, kdup, kdup, vt, vt, vt)


def _segment_flags(t, ctx_tiles, n_tiles):
    first = (t == 0) | (t == ctx_tiles)
    last = (t == ctx_tiles - 1) | (t == n_tiles - 1)
    return first, last


def _rwkv_prep_kernel(f_ref, fp_ref, fn_ref, mup_ref, mun_ref, w0_ref, w2_ref, a0_ref, a2_ref, g2_ref,
                      kk_w_ref, ka_ref, rk_ref, seg_ref,
                      r_ref, v_ref, kk_ref, lw_ref, bb_ref, kd_ref, bonus_ref, g_ref,
                      *, ctx_tiles, n_tiles):
    first, last = _segment_flags(pl.program_id(1), ctx_tiles, n_tiles)
    cur = f_ref[0].astype(F32)
    prev8 = jnp.where(first, 0.0, fp_ref[0].astype(F32)[HALO_BF16 - HALO:])
    next8 = jnp.where(last, 0.0, fn_ref[0].astype(F32)[:HALO])
    prv = _shift_rows(cur, prev8, next8, 1)
    nxt = _shift_rows(cur, prev8, next8, -1)
    f = cur + mup_ref[...] * (prv - cur) + mun_ref[...] * (nxt - cur)
    r = f[:, 0:MIX_W]
    k = f[:, MIX_W:2 * MIX_W]
    v = f[:, 2 * MIX_W:3 * MIX_W]
    wd = f[:, 3 * MIX_W:3 * MIX_W + B_LORA_W]
    ad = f[:, 3 * MIX_W + B_LORA_W:3 * MIX_W + 2 * B_LORA_W]
    gd = f[:, 3 * MIX_W + 2 * B_LORA_W:]
    wlog = w0_ref[...] + _dot(jnp.tanh(wd), w2_ref[...])
    lw_ref[0] = -math.exp(-0.5) * _sigmoid(wlog)
    a = _sigmoid(a0_ref[...] + _dot(ad, a2_ref[...]))
    kk = k * kk_w_ref[...]
    kk = kk * lax.rsqrt(_dot(kk * kk, seg_ref[...]) + 1e-12)
    ksum = jnp.zeros_like(k)
    for d in range(2):
        a_d = a[:, d * MIX_W:(d + 1) * MIX_W]
        k_d = k * (1.0 + (a_d - 1.0) * ka_ref[...])
        bb_ref[0, :, d * MIX_W:(d + 1) * MIX_W] = (a_d * kk).astype(bb_ref.dtype)
        kd_ref[0, :, d * MIX_W:(d + 1) * MIX_W] = k_d.astype(kd_ref.dtype)
        ksum = ksum + k_d
    r_ref[0] = r.astype(r_ref.dtype)
    v_ref[0] = v.astype(v_ref.dtype)
    kk_ref[0] = kk.astype(kk_ref.dtype)
    bonus_ref[0] = (_dot_exact_rhs(r * ksum * rk_ref[...], seg_ref[...]) * v).astype(bonus_ref.dtype)
    g_ref[0] = _dot(_sigmoid(gd), g2_ref[...]).astype(g_ref.dtype)


def _rwkv_prep_call(f, p, ctx_len, tm):
    bsz, t_all, _ = f.shape
    nt = t_all // tm
    ctx_tiles = ctx_len // tm
    hb = HALO_BF16
    n_halo = t_all // hb
    const = lambda b, t: (0, 0)
    tok = lambda b, t: (b, t, 0)
    wide = jax.ShapeDtypeStruct((bsz, t_all, 2 * MIX_W), BF16)
    one = jax.ShapeDtypeStruct((bsz, t_all, MIX_W), BF16)
    logs = jax.ShapeDtypeStruct((bsz, t_all, 2 * MIX_W), F32)
    row = lambda w: pl.BlockSpec((1, w), const)
    return pl.pallas_call(
        functools.partial(_rwkv_prep_kernel, ctx_tiles=ctx_tiles, n_tiles=nt),
        grid=(bsz, nt),
        in_specs=[pl.BlockSpec((1, tm, B_IN), tok),
                  pl.BlockSpec((1, hb, B_IN), lambda b, t: (b, jnp.maximum(t * (tm // hb) - 1, 0), 0)),
                  pl.BlockSpec((1, hb, B_IN), lambda b, t: (b, jnp.minimum((t + 1) * (tm // hb), n_halo - 1), 0)),
                  row(B_IN), row(B_IN),
                  row(2 * MIX_W), pl.BlockSpec((B_LORA_W, 2 * MIX_W), const),
                  row(2 * MIX_W), pl.BlockSpec((B_LORA_W, 2 * MIX_W), const),
                  pl.BlockSpec((B_LORA_W, MIX_W), const),
                  row(MIX_W), row(MIX_W), row(MIX_W),
                  pl.BlockSpec((MIX_W, MIX_W), const)],
        out_specs=[pl.BlockSpec((1, tm, MIX_W), tok)] * 3 + [pl.BlockSpec((1, tm, 2 * MIX_W), tok)] * 3
                  + [pl.BlockSpec((1, tm, MIX_W), tok)] * 2,
        out_shape=[one, one, one, logs, wide, wide, one, one],
        compiler_params=_cparams(("parallel", "arbitrary")),
        name="rwkv_prep",
    )(f, f, f, p["mu_prev"], p["mu_next"], p["w0"], p["w2"], p["a0"], p["a2"], p["g2"],
      p["k_k"], p["k_a"], p["r_k"], p["seg"])


def _scan_chunk(s, d, n_ctx, n_all):
    bwd = jnp.where(s < n_ctx, n_ctx - 1 - s, n_all - 1 + n_ctx - s)
    return jnp.where(d == 0, s, bwd)


def _block_diag2(x):
    lane = lax.broadcasted_iota(jnp.int32, x.shape, 1)
    zero = jnp.zeros_like(x)
    return jnp.concatenate([jnp.where(lane < HEAD_DIM, x, zero), jnp.where(lane < HEAD_DIM, zero, x)], axis=0)


def _rwkv_scan_kernel(rf_ref, vf_ref, kkf_ref, lwf_ref, bbf_ref, kdf_ref,
                      rb_ref, vb_ref, kkb_ref, lwb_ref, bbb_ref, kdb_ref,
                      yf_ref, yb_ref, state_ref):
    c = rf_ref.shape[1]

    @pl.when(pl.program_id(1) == 0)
    def _():
        state_ref[...] = jnp.zeros_like(state_ref)

    ri = lax.broadcasted_iota(jnp.int32, (c, c), 0)
    ci = lax.broadcasted_iota(jnp.int32, (c, c), 1)
    ri2 = lax.broadcasted_iota(jnp.int32, (c, 2 * c), 0)
    ci2 = lax.broadcasted_iota(jnp.int32, (c, 2 * c), 1) & (c - 1)
    eye2 = jnp.where(ri2 == ci2, 1.0, 0.0)

    units = []
    dir_refs = ((rf_ref, vf_ref, kkf_ref, lwf_ref, bbf_ref, kdf_ref),
                (rb_ref, vb_ref, kkb_ref, lwb_ref, bbb_ref, kdb_ref))
    for d, bi in [(d, bi) for d in range(2) for bi in range(rf_ref.shape[0])]:
        r_ref, v_ref, kk_ref, lw_ref, bb_ref, kd_ref = dir_refs[d]
        incl = (ri - ci if d == 0 else ci - ri) >= 0
        lw = lw_ref[bi]
        cum = _dot_exact_lhs(jnp.where(incl, 1.0, 0.0).astype(BF16), lw)
        e_cum = jnp.exp(cum)
        e_inv = jnp.exp(-cum)
        e_tot = jnp.exp(jnp.sum(lw, axis=0, keepdims=True))
        qt = r_ref[bi] * e_cum
        kkt = kk_ref[bi] * jnp.exp(cum - lw)
        kh = kd_ref[bi] * e_inv
        bh = bb_ref[bi] * e_inv
        k_end = kh * e_tot
        b_end = -(bh * e_tot)
        v = v_ref[bi].astype(F32)
        incl2 = (ri2 - ci2 if d == 0 else ci2 - ri2) >= 0
        strict2 = (ri2 - ci2 if d == 0 else ci2 - ri2) > 0
        for j in range(N_HEADS // 2):
            sl = slice(2 * HEAD_DIM * j, 2 * HEAD_DIM * (j + 1))
            units.append(dict(
                d=d, bi=bi, j=j, sl=sl, incl=incl2, strict=strict2, v=v[:, sl], e_tot=e_tot[:, sl],
                lhs=jnp.concatenate([qt[:, sl], kkt[:, sl]], axis=0).astype(BF16),
                rhs=jnp.concatenate([_block_diag2(kh[:, sl].astype(BF16)),
                                     _block_diag2(bh[:, sl].astype(BF16))], axis=0),
                x_end=jnp.concatenate([k_end[:, sl], b_end[:, sl]], axis=0).astype(BF16)))

    c2 = 2 * c
    for u in units:
        a = _dot_nt(u["lhs"], u["rhs"])
        u["a_v"] = jnp.concatenate([jnp.where(u["incl"], a[:c, :c2], 0.0),
                                    jnp.where(u["strict"], a[c:, :c2], 0.0)], axis=0).astype(BF16)
        u["a_rb"] = jnp.where(u["incl"], a[:c, c2:], 0.0).astype(BF16)
        u["p"] = -jnp.where(u["strict"], a[c:, c2:], 0.0)
    for u in units:
        pb = u["p"].astype(BF16)
        u["t"] = eye2 + u["p"]
        u["p"] = jnp.dot(pb, _block_diag2(pb), preferred_element_type=F32)
    m = 2
    while 2 * m < c:
        for u in units:
            pb = u["p"].astype(BF16)
            both = jnp.dot(jnp.concatenate([u["t"].astype(BF16), pb], axis=0), _block_diag2(pb),
                           preferred_element_type=F32)
            u["t"] = u["t"] + both[:c]
            u["p"] = both[c:]
        m *= 2
    for u in units:
        u["t"] = (u["t"] + _dot(u["t"], _block_diag2(u["p"].astype(BF16)))).astype(BF16)
        u["av"] = _dot(u["a_v"], _block_diag2(u["v"].astype(BF16)))
    for u in units:
        u["st"] = state_ref[u["d"], u["bi"], u["j"]]
        u["x1"] = _dot_nt(u["lhs"], u["st"]) + u["av"]
    for u in units:
        u["u"] = _dot(u["t"], _block_diag2(u["x1"][c:].astype(BF16)))
    rr = lax.broadcasted_iota(jnp.int32, (c2, c2), 0)
    cc = lax.broadcasted_iota(jnp.int32, (c2, c2), 1)
    same_head = (rr // HEAD_DIM) == (cc // HEAD_DIM)
    for u in units:
        y = u["x1"][:c] - _dot(u["a_rb"], _block_diag2(u["u"].astype(BF16)))
        (yf_ref, yb_ref)[u["d"]][u["bi"], :, u["sl"]] = y.astype(yf_ref.dtype)
        wv = jnp.concatenate([u["v"], u["u"]], axis=0)
        upd = jnp.where(same_head, _dot(wv.T, u["x_end"]), 0.0)
        state_ref[u["d"], u["bi"], u["j"]] = u["st"] * u["e_tot"] + upd


def _rwkv_scan_call(r, v, kk, lw, bb, kd, ctx_len):
    bsz, t_all, _ = r.shape
    c = RWKV_CHUNK
    nb = RWKV_SAMPLES_PER_STEP if bsz % RWKV_SAMPLES_PER_STEP == 0 else 1
    n_all = t_all // c
    n_ctx = ctx_len // c
    blk = lambda im: pl.BlockSpec((nb, c, MIX_W), im)
    fwd = lambda b, s: (b, s, 0)
    bwd = lambda b, s: (b, _scan_chunk(s, 1, n_ctx, n_all), 0)
    bwd1 = lambda b, s: (b, _scan_chunk(s, 1, n_ctx, n_all), 1)
    out = jax.ShapeDtypeStruct((bsz, t_all, MIX_W), BF16)
    return pl.pallas_call(
        _rwkv_scan_kernel,
        grid=(bsz // nb, n_all),
        in_specs=[blk(fwd)] * 6 + [blk(bwd)] * 3 + [blk(bwd1)] * 3,
        out_specs=[blk(fwd), blk(bwd)],
        out_shape=[out, out],
        scratch_shapes=[pltpu.VMEM((2, nb, N_HEADS // 2, 2 * HEAD_DIM, 2 * HEAD_DIM), F32)],
        compiler_params=_cparams(("parallel", "arbitrary")),
        name="rwkv_scan",
    )(r, v, kk, lw, bb, kd, r, v, kk, lw, bb, kd)


def _conv_kernel(x_ref, xp_ref, xn_ref, w_ref, b_ref, o_ref, *, ctx_tiles, n_tiles):
    first, last = _segment_flags(pl.program_id(1), ctx_tiles, n_tiles)
    cur = x_ref[0].astype(F32)
    prev8 = jnp.where(first, 0.0, xp_ref[0].astype(F32)[HALO_BF16 - HALO:])
    next8 = jnp.where(last, 0.0, xn_ref[0].astype(F32)[:HALO])
    acc = b_ref[...] + w_ref[D_CONV // 2:D_CONV // 2 + 1, :] * cur
    for kpos in range(D_CONV):
        s = D_CONV // 2 - kpos
        if s != 0:
            acc = acc + w_ref[kpos:kpos + 1, :] * _shift_rows(cur, prev8, next8, s)
    o_ref[0] = (acc * _sigmoid(acc)).astype(o_ref.dtype)


def _conv_call(xbc, conv_w, conv_b, ctx_len, tm):
    bsz, t_all, w = xbc.shape
    nt = t_all // tm
    hb = HALO_BF16
    n_halo = t_all // hb
    return pl.pallas_call(
        functools.partial(_conv_kernel, ctx_tiles=ctx_len // tm, n_tiles=nt),
        grid=(bsz, nt),
        in_specs=[pl.BlockSpec((1, tm, w), lambda b, t: (b, t, 0)),
                  pl.BlockSpec((1, hb, w), lambda b, t: (b, jnp.maximum(t * (tm // hb) - 1, 0), 0)),
                  pl.BlockSpec((1, hb, w), lambda b, t: (b, jnp.minimum((t + 1) * (tm // hb), n_halo - 1), 0)),
                  pl.BlockSpec((D_CONV, w), lambda b, t: (0, 0)),
                  pl.BlockSpec((1, w), lambda b, t: (0, 0))],
        out_specs=pl.BlockSpec((1, tm, w), lambda b, t: (b, t, 0)),
        out_shape=jax.ShapeDtypeStruct((bsz, t_all, w), BF16),
        compiler_params=_cparams(("parallel", "arbitrary")),
        name="ssd_conv",
    )(xbc, xbc, xbc, conv_w, conv_b)


def _ssd_scan_kernel(uf_ref, dtf_ref, ub_ref, dtb_ref, bias_ref, a_ref, sel_ref, exp_ref,
                     yf_ref, yb_ref, state_ref):
    c = uf_ref.shape[1]

    @pl.when(pl.program_id(1) == 0)
    def _():
        state_ref[...] = jnp.zeros_like(state_ref)

    ri = lax.broadcasted_iota(jnp.int32, (c, c), 0)
    ci = lax.broadcasted_iota(jnp.int32, (c, c), 1)
    expand = exp_ref[...]
    gw = D_STATE
    gl = MIX_W // D_GROUPS
    hpg = N_HEADS // D_GROUPS
    dirs = []
    for d, bi in [(d, bi) for d in range(2) for bi in range(uf_ref.shape[0])]:
        u_ref, dt_ref = ((uf_ref, dtf_ref), (ub_ref, dtb_ref))[d]
        incl = (ri - ci if d == 0 else ci - ri) >= 0
        incl_m = jnp.where(incl, 1.0, 0.0).astype(BF16)
        dt_all = _softplus(dt_ref[bi] + bias_ref[...])
        dt_d = _dot_exact_rhs(dt_all, sel_ref[d])
        dta_d = _dot_exact_rhs(dt_all * a_ref[...], sel_ref[d])
        cum_d = _dot_exact_lhs(incl_m, dta_d)
        dta_x = _dot_exact_rhs(dta_d, expand)
        dirs.append(dict(d=d, bi=bi, u_ref=u_ref, incl=incl, cum_d=cum_d, cum_t=cum_d.T, dt_t=dt_d.T,
                         cum_x=_dot_exact_lhs(incl_m, dta_x), dt_x=_dot_exact_rhs(dt_d, expand),
                         tot_x=jnp.sum(dta_x, axis=0, keepdims=True), xs=u_ref[bi, :, 0:MIX_W], ys=[]))

    def b_mat(p, g):
        return p["u_ref"][p["bi"], :, MIX_W + gw * g:MIX_W + gw * (g + 1)]

    def c_mat(p, g):
        return p["u_ref"][p["bi"], :, MIX_W + gw * D_GROUPS + gw * g:MIX_W + gw * D_GROUPS + gw * (g + 1)]

    def head_matrix(p, h):
        seg = p["cum_d"][:, h:h + 1] - p["cum_t"][h:h + 1, :]
        lmat = jnp.exp(jnp.where(p["incl"], seg, -jnp.inf))
        return (p["cb"] * lmat * p["dt_t"][h:h + 1, :]).astype(BF16)

    for g in range(D_GROUPS):
        for p in dirs:
            p["cb"] = _dot_nt(c_mat(p, g), b_mat(p, g))
        for r in range(0, hpg, 2):
            h = g * hpg + r
            for p in dirs:
                mm = jnp.concatenate([head_matrix(p, h), head_matrix(p, h + 1)], axis=1)
                x_pair = p["xs"][:, HEAD_DIM * h:HEAD_DIM * (h + 2)]
                p["ys"].append(jnp.dot(mm, _block_diag2(x_pair), preferred_element_type=F32))

    for p in dirs:
        d, bi = p["d"], p["bi"]
        wj = jnp.exp(p["tot_x"] - p["cum_x"]) * p["dt_x"]
        xw = p["xs"] * wj
        e_tot = jnp.exp(p["tot_x"])
        offs = []
        for g in range(D_GROUPS):
            st = state_ref[d, bi, :, gl * g:gl * (g + 1)]
            offs.append(_dot(c_mat(p, g), st))
            bm_t = b_mat(p, g).astype(F32).T
            state_ref[d, bi, :, gl * g:gl * (g + 1)] = (st * e_tot[:, gl * g:gl * (g + 1)]
                                                        + _dot(bm_t, xw[:, gl * g:gl * (g + 1)]))
        (yf_ref, yb_ref)[d][bi] = (jnp.concatenate(p["ys"], axis=1)
                                   + jnp.concatenate(offs, axis=1) * jnp.exp(p["cum_x"])).astype(yf_ref.dtype)


def _ssd_scan_call(u, dt_raw, dtb, a_row, sel, expand, ctx_len):
    bsz, t_all, w = u.shape
    c = SSD_CHUNK
    nb = SSD_SAMPLES_PER_STEP if bsz % SSD_SAMPLES_PER_STEP == 0 else 1
    n_all = t_all // c
    n_ctx = ctx_len // c
    fwd = lambda b, s: (b, s, 0)
    bwd = lambda b, s: (b, _scan_chunk(s, 1, n_ctx, n_all), 0)
    const = lambda b, s: (0, 0)
    out = jax.ShapeDtypeStruct((bsz, t_all, MIX_W), BF16)
    return pl.pallas_call(
        _ssd_scan_kernel,
        grid=(bsz // nb, n_all),
        in_specs=[pl.BlockSpec((nb, c, w), fwd), pl.BlockSpec((nb, c, DT_PAD), fwd),
                  pl.BlockSpec((nb, c, w), bwd), pl.BlockSpec((nb, c, DT_PAD), bwd),
                  pl.BlockSpec((1, DT_PAD), const),
                  pl.BlockSpec((1, DT_PAD), const),
                  pl.BlockSpec((2, DT_PAD, DT_PAD), lambda b, s: (0, 0, 0)),
                  pl.BlockSpec((DT_PAD, MIX_W), const)],
        out_specs=[pl.BlockSpec((nb, c, MIX_W), fwd), pl.BlockSpec((nb, c, MIX_W), bwd)],
        out_shape=[out, out],
        scratch_shapes=[pltpu.VMEM((2, nb, D_STATE, MIX_W), F32)],
        compiler_params=_cparams(("parallel", "arbitrary")),
        name="ssd_scan",
    )(u, dt_raw, u, dt_raw, dtb, a_row, sel, expand)


def _rwkv_mix(yf_ref, yb_ref, bonus_ref, g_ref, lnw_ref, lnb_ref, seg_ref):
    y = yf_ref[0].astype(F32) + yb_ref[0].astype(F32)
    inv = 1.0 / HEAD_DIM
    mu = _dot_exact_rhs(y, seg_ref[...]) * inv
    yc = y - mu
    var = _dot(yc * yc, seg_ref[...]) * inv
    yn = yc * lax.rsqrt(var + B_GN_EPS) * lnw_ref[...] + lnb_ref[...]
    return (yn + bonus_ref[0]) * g_ref[0]


def _ssd_mix(yf_ref, yb_ref, u_ref, z_ref, dsk_ref, nw_ref):
    z = z_ref[0].astype(F32)
    y = ((yf_ref[0].astype(F32) + yb_ref[0].astype(F32) + dsk_ref[...] * u_ref[0].astype(F32))
         * (z * _sigmoid(z)))
    gl = MIX_W // D_GROUPS
    parts = []
    for g in range(D_GROUPS):
        yg = y[:, gl * g:gl * (g + 1)]
        parts.append(yg * lax.rsqrt(jnp.mean(yg * yg, axis=-1, keepdims=True) + NORM_EPS))
    return jnp.concatenate(parts, axis=1) * nw_ref[...]


def _tail_kernel(x_ref, mod_ref, att_ref, *refs, kind, final):
    n_mix = 7 if kind == "rwkv" else 6
    mix = (_rwkv_mix if kind == "rwkv" else _ssd_mix)(*refs[:n_mix])
    wo_ref, nw_ref, w1_ref, w2_ref, fw_ref, o_ref = refs[n_mix:]
    out = (jnp.dot(att_ref[0], wo_ref[0:MIX_W, :], preferred_element_type=F32)
           + jnp.dot(mix.astype(BF16), wo_ref[MIX_W:, :], preferred_element_type=F32))
    x = x_ref[0] + mod_ref[0, 2:3, :] * out
    ms = jnp.mean(x * x, axis=-1, keepdims=True)
    h = x * lax.rsqrt(ms + NORM_EPS) * nw_ref[...]
    hb = (h * (1.0 + mod_ref[0, 4:5, :]) + mod_ref[0, 3:4, :]).astype(BF16)
    acc = jnp.zeros_like(x)
    for j in range(MLP_HIDDEN // D_MODEL):
        a = jnp.maximum(jnp.dot(hb, w1_ref[:, D_MODEL * j:D_MODEL * (j + 1)], preferred_element_type=F32), 0.0)
        acc = acc + jnp.dot((a * a).astype(BF16), w2_ref[D_MODEL * j:D_MODEL * (j + 1), :],
                            preferred_element_type=F32)
    x = x + mod_ref[0, 5:6, :] * acc
    if final:
        x = x * lax.rsqrt(jnp.mean(x * x, axis=-1, keepdims=True) + NORM_EPS) * fw_ref[...]
    o_ref[0] = x


def _tail_call(kind, x_all, mod, att, extras, rows, w_out, norm_w, w1, w2, final_w,
               n_batch, ctx_len, tm, lat_only, final):
    bsz, t_all, _ = x_all.shape
    ctx_tiles = ctx_len // tm
    off = ctx_tiles if lat_only else 0
    nt = t_all // tm - off
    tok = lambda b, t: (b, t + off, 0)
    att_off = 0 if att.shape[1] == t_all else -off
    const = lambda b, t: (0, 0)
    weights = lambda shape: pl.BlockSpec(shape, const, pipeline_mode=pl.Buffered(1))
    in_specs = [pl.BlockSpec((1, tm, D_MODEL), tok),
                _mod_spec(n_batch, ctx_tiles, off),
                pl.BlockSpec((1, tm, MIX_W), lambda b, t: (b, t + off + att_off, 0))]
    in_specs += [pl.BlockSpec((1, tm, MIX_W), tok) for _ in extras]
    in_specs += [pl.BlockSpec(r.shape, const) for r in rows]
    in_specs += [weights((D_MODEL, D_MODEL)),
                 pl.BlockSpec((1, D_MODEL), const),
                 weights((D_MODEL, MLP_HIDDEN)),
                 weights((MLP_HIDDEN, D_MODEL)),
                 pl.BlockSpec((1, D_MODEL), const)]
    return pl.pallas_call(
        functools.partial(_tail_kernel, kind=kind, final=final),
        grid=(bsz, nt),
        in_specs=in_specs,
        out_specs=pl.BlockSpec((1, tm, D_MODEL), lambda b, t: (b, t, 0)),
        out_shape=jax.ShapeDtypeStruct((bsz, nt * tm, D_MODEL), F32),
        compiler_params=_cparams(("parallel", "arbitrary")),
        name="tail_" + kind + ("_final" if final else ""),
    )(x_all, mod, att, *extras, *rows, w_out, norm_w, w1, w2, final_w)


def _rope_tables(n_lat, ctx_len):
    rows = n_lat // GRID_W
    row = jnp.repeat(jnp.arange(rows, dtype=F32), GRID_W)
    col = jnp.tile(jnp.arange(GRID_W, dtype=F32), rows)
    n_freq = HEAD_DIM // 4
    inv_freq = ROPE_THETA ** (-jnp.arange(n_freq, dtype=F32) / n_freq)
    ang_r = row[:, None] * inv_freq
    ang_c = col[:, None] * inv_freq
    cr, sr, cc, sc = jnp.cos(ang_r), jnp.sin(ang_r), jnp.cos(ang_c), jnp.sin(ang_c)
    cos = jnp.concatenate([cr, cr, cc, cc], axis=-1)
    sin = jnp.concatenate([-sr, sr, -sc, sc], axis=-1)
    cos = jnp.concatenate([jnp.ones((ctx_len, HEAD_DIM), F32), cos], axis=0)
    sin = jnp.concatenate([jnp.zeros((ctx_len, HEAD_DIM), F32), sin], axis=0)
    return jnp.tile(cos, (1, N_HEADS)), jnp.tile(sin, (1, N_HEADS))


def _dup_kv_cols(w):
    k0, k1 = w[:, :HEAD_DIM], w[:, HEAD_DIM:]
    return jnp.concatenate([k0, k0, k1, k1], axis=1)


def _seg_matrix():
    return jnp.asarray(np.kron(np.eye(N_HEADS), np.ones((HEAD_DIM, HEAD_DIM))), BF16)


def _tile_rows(ctx_len):
    return 256 if ctx_len % 256 == 0 else 128


def _forward(x, c, ctx, c_ctx, w_mod, b_mod, norm_mix, norm_mlp, w_out, mlp_w1, mlp_w2, final_norm,
             ab_w_in, a_q_norm, a_k_norm, b_mu_prev, b_mu_next, b_w0, b_w2, b_a0, b_a2, b_g2,
             b_k_k, b_k_a, b_r_k, b_ln_w, b_ln_b,
             cd_w_in, c_sink, d_conv_w, d_conv_b, d_dt_bias, d_A_log, d_D, d_norm_w):
    bsz, n_lat, _ = x.shape
    ctx_len = ctx.shape[1]
    tm = _tile_rows(ctx_len)
    ctx_tiles = ctx_len // tm
    mod_rows = -(-(bsz + 1) // 8) * 8
    c_all = jnp.concatenate([c, c_ctx[None, :], jnp.zeros((mod_rows - bsz - 1, D_MODEL), F32)], axis=0)
    mods = _mod_call(c_all, w_mod, b_mod).reshape(DEPTH, mod_rows, 6, D_MODEL)
    cos, sin = _rope_tables(n_lat, ctx_len)
    seg = _seg_matrix()
    row = lambda a: a.reshape(1, -1).astype(F32)
    x_all = jnp.concatenate([ctx, x], axis=1)
    head_dup = lambda w, n: jnp.tile(w.astype(F32), n).reshape(1, -1)

    for i in range(DEPTH):
        j = i // 2
        last = i == DEPTH - 1
        mod = mods[i]
        if i % 2 == 0:
            w = ab_w_in[j]
            w_in = jnp.concatenate([w[:, :MIX_W], _dup_kv_cols(w[:, MIX_W:MIX_W + 128]), w[:, MIX_W + 128:]],
                                   axis=1).astype(BF16)
            q, kdup, vt, f = _premix_call(
                x_all, mod, row(norm_mix[i]), w_in, cos, sin, head_dup(a_q_norm[j], N_HEADS),
                head_dup(a_k_norm[j], 2 * KV_HEADS), seg, (MIX_W, KV_DUP_W, KV_W, B_IN),
                (BF16, BF16, BF16, BF16), True, bsz, ctx_tiles, tm)
            att = _gqa_call(q, kdup, vt, ctx_len)
            zl = jnp.zeros((B_LORA_W // 2, MIX_W), F32)
            dir_cat = lambda w: jnp.concatenate([jnp.concatenate([w[0], zl], axis=1),
                                                 jnp.concatenate([zl, w[1]], axis=1)], axis=0)
            w2cat = dir_cat(b_w2[j])
            a2cat = dir_cat(b_a2[j])
            p = dict(mu_prev=row(b_mu_prev[j]), mu_next=row(b_mu_next[j]), w0=row(b_w0[j]),
                     w2=w2cat.astype(BF16), a0=row(b_a0[j]), a2=a2cat.astype(BF16),
                     g2=b_g2[j].astype(BF16), k_k=row(b_k_k[j]), k_a=row(b_k_a[j]), r_k=row(b_r_k[j]),
                     seg=seg)
            r, v, kk, lw, bb, kd, bonus, g = _rwkv_prep_call(f, p, ctx_len, tm)
            yf, yb = _rwkv_scan_call(r, v, kk, lw, bb, kd, ctx_len)
            kind, extras, rows = "rwkv", [yf, yb, bonus, g], [row(b_ln_w[j]), row(b_ln_b[j]), seg]
        else:
            w = cd_w_in[j]
            w_in = jnp.concatenate([w[:, :MIX_W], _dup_kv_cols(w[:, MIX_W:MIX_W + 128]), w[:, MIX_W + 128:],
                                    jnp.zeros((D_MODEL, DT_PAD - 2 * N_HEADS), F32)], axis=1).astype(BF16)
            q, kdup, vt, z, xbc, dt_raw = _premix_call(
                x_all, mod, row(norm_mix[i]), w_in, cos, sin, jnp.ones((1, MIX_W), F32),
                jnp.ones((1, KV_DUP_W), F32), seg, (MIX_W, KV_DUP_W, KV_W, MIX_W, D_XBC, DT_PAD),
                (BF16, BF16, BF16, BF16, BF16, F32), False, bsz, ctx_tiles, tm)
            att = _swa_call(c_sink[j].astype(F32), q, kdup, vt, ctx_len, last)
            u = _conv_call(xbc, d_conv_w[j], row(d_conv_b[j]), ctx_len, tm)
            pad = jnp.zeros((DT_PAD - 2 * N_HEADS,), F32)
            dtb = jnp.concatenate([d_dt_bias[j].reshape(-1), pad]).reshape(1, -1)
            a_row = jnp.concatenate([-jnp.exp(d_A_log[j].astype(F32)).reshape(-1), pad]).reshape(1, -1)
            sel = np.zeros((2, DT_PAD, DT_PAD), np.float32)
            expand = np.zeros((DT_PAD, MIX_W), np.float32)
            for h in range(N_HEADS):
                sel[0, h, h] = 1.0
                sel[1, N_HEADS + h, h] = 1.0
                expand[h, HEAD_DIM * h:HEAD_DIM * (h + 1)] = 1.0
            yf, yb = _ssd_scan_call(u, dt_raw, dtb, a_row, jnp.asarray(sel, BF16), jnp.asarray(expand, BF16), ctx_len)
            dsk = jnp.repeat(d_D[j].astype(F32), HEAD_DIM).reshape(1, -1)
            kind, extras, rows = "ssd", [yf, yb, u, z], [dsk, row(d_norm_w[j])]
        x_all = _tail_call(kind, x_all, mod, att, extras, rows, w_out[i].astype(BF16), row(norm_mlp[i]),
                           mlp_w1[i].astype(BF16), mlp_w2[i].astype(BF16), row(final_norm),
                           bsz, ctx_len, tm, last, last)
    return x_all


def kernel(x, c, ctx, c_ctx, w_mod, b_mod, norm_mix, norm_mlp, w_out, mlp_w1, mlp_w2, final_norm, ab_w_in, a_q_norm, a_k_norm, b_mu_prev, b_mu_next, b_w0, b_w2, b_a0, b_a2, b_g2, b_k_k, b_k_a, b_r_k, b_ln_w, b_ln_b, cd_w_in, c_sink, d_conv_w, d_conv_b, d_dt_bias, d_A_log, d_D, d_norm_w):
    return _forward(x, c, ctx, c_ctx, w_mod, b_mod, norm_mix, norm_mlp, w_out, mlp_w1, mlp_w2, final_norm,
                    ab_w_in, a_q_norm, a_k_norm, b_mu_prev, b_mu_next, b_w0, b_w2, b_a0, b_a2, b_g2,
                    b_k_k, b_k_a, b_r_k, b_ln_w, b_ln_b,
                    cd_w_in, c_sink, d_conv_w, d_conv_b, d_dt_bias, d_A_log, d_D, d_norm_w)
```

```python
import functools
import math

import numpy as np
import jax
import jax.numpy as jnp
from jax import lax
from jax.experimental import pallas as pl
from jax.experimental.pallas import tpu as pltpu

F32 = jnp.float32
BF16 = jnp.bfloat16

D_MODEL = 1024
DEPTH = 4
GRID_W = 64
HEAD_DIM = 64
ROPE_THETA = 10000.0
NORM_EPS = 1e-6
MLP_HIDDEN = 4 * D_MODEL
N_HEADS = 8
MIX_W = N_HEADS * HEAD_DIM
KV_HEADS = 2
KV_W = KV_HEADS * HEAD_DIM
KV_DUP_W = 2 * KV_W
B_LORA_W = 128
B_IN = 3 * MIX_W + 3 * B_LORA_W
B_GN_EPS = 64e-5
WINDOW = 128
D_STATE = 128
D_GROUPS = 2
D_CONV = 5
D_XBC = MIX_W + 2 * D_GROUPS * D_STATE
DT_PAD = 128
SSD_CHUNK = 128
SSD_SAMPLES_PER_STEP = 4
RWKV_CHUNK = 64
RWKV_SAMPLES_PER_STEP = 8
ATT_TQ = 128
GQA_TQ = 256
LOG2E = math.log2(math.e)
Q_SCALE = HEAD_DIM ** -0.5 * LOG2E
ONES_ROWS = 16
HALO = 8
HALO_BF16 = 16
VMEM_LIMIT = 56 * 1024 * 1024


def _cparams(sem):
    return pltpu.CompilerParams(dimension_semantics=sem, vmem_limit_bytes=VMEM_LIMIT)


def _dot(a, b):
    return jnp.dot(a.astype(BF16), b.astype(BF16), preferred_element_type=F32)


def _dot_nt(a, b):
    return lax.dot_general(a.astype(BF16), b.astype(BF16), (((1,), (1,)), ((), ())),
                           preferred_element_type=F32)


def _split_hi_lo(x):
    hi = x.astype(BF16)
    lo = (x - hi.astype(F32)).astype(BF16)
    return hi, lo


def _dot_exact_rhs(x, m):
    hi, lo = _split_hi_lo(x)
    return (jnp.dot(hi, m, preferred_element_type=F32)
            + jnp.dot(lo, m, preferred_element_type=F32))


def _dot_exact_lhs(m, x):
    hi, lo = _split_hi_lo(x)
    return (jnp.dot(m, hi, preferred_element_type=F32)
            + jnp.dot(m, lo, preferred_element_type=F32))


def _rope(x, cos, sin_signed):
    w = x.shape[-1]
    lane = lax.broadcasted_iota(jnp.int32, x.shape, 1)
    partner = jnp.where((lane & 16) == 0, pltpu.roll(x, w - 16, 1), pltpu.roll(x, 16, 1))
    return x * cos + partner * sin_signed


def _shift_rows(cur, prev8, next8, s):
    n = cur.shape[0]
    if s == 0:
        return cur
    ri = lax.broadcasted_iota(jnp.int32, (HALO, cur.shape[1]), 0)
    if s > 0:
        rolled = pltpu.roll(cur, s, 0)
        fill = pltpu.roll(prev8, s, 0)
        head = jnp.where(ri < s, fill, rolled[:HALO])
        return jnp.concatenate([head, rolled[HALO:]], axis=0)
    a = -s
    rolled = pltpu.roll(cur, n - a, 0)
    fill = pltpu.roll(next8, HALO - a, 0)
    tail = jnp.where(ri >= HALO - a, fill, rolled[n - HALO:])
    return jnp.concatenate([rolled[:n - HALO], tail], axis=0)


def _sigmoid(x):
    return 0.5 * jnp.tanh(0.5 * x) + 0.5


def _softplus(x):
    return jnp.maximum(x, 0.0) + jnp.log(1.0 + jnp.exp(-jnp.abs(x)))


def _mod_kernel(c_ref, w_ref, b_ref, o_ref):
    c = c_ref[...]
    o_ref[0] = _dot(c * _sigmoid(c), w_ref[0]) + b_ref[0]


def _mod_call(c_all, w_mod, b_mod):
    rows = c_all.shape[0]
    n_col = w_mod.shape[2] // D_MODEL
    return pl.pallas_call(
        _mod_kernel,
        grid=(DEPTH, n_col),
        in_specs=[pl.BlockSpec((rows, D_MODEL), lambda i, j: (0, 0)),
                  pl.BlockSpec((1, D_MODEL, D_MODEL), lambda i, j: (i, 0, j)),
                  pl.BlockSpec((1, 1, D_MODEL), lambda i, j: (i, 0, j))],
        out_specs=pl.BlockSpec((1, rows, D_MODEL), lambda i, j: (i, 0, j)),
        out_shape=jax.ShapeDtypeStruct((DEPTH, rows, w_mod.shape[2]), F32),
        compiler_params=_cparams(("arbitrary", "arbitrary")),
        name="modulation",
    )(c_all, w_mod, b_mod.reshape(DEPTH, 1, -1))


def _mod_spec(n_batch, ctx_tiles, tile_off):
    return pl.BlockSpec((1, 6, D_MODEL),
                        lambda b, t: (jnp.where(t + tile_off < ctx_tiles, n_batch, b), 0, 0))


def _head_rms(y, seg):
    w = y.shape[-1]
    ssum = _dot(y * y, seg[:w, :w])
    return y * lax.rsqrt(ssum * (1.0 / HEAD_DIM) + NORM_EPS)


def _premix_kernel(x_ref, mod_ref, nw_ref, w_ref, cos_ref, sin_ref, qnw_ref, knw_ref, seg_ref,
                   *out_refs, widths, qk_norm):
    x = x_ref[0]
    ms = jnp.mean(x * x, axis=-1, keepdims=True)
    h = x * lax.rsqrt(ms + NORM_EPS) * nw_ref[...]
    h = h * (1.0 + mod_ref[0, 1:2, :]) + mod_ref[0, 0:1, :]
    hb = h.astype(BF16)
    off = 0
    for idx, (o_ref, wdt) in enumerate(zip(out_refs, widths)):
        y = jnp.dot(hb, w_ref[:, off:off + wdt], preferred_element_type=F32)
        if idx < 2:
            if qk_norm:
                nw = qnw_ref if idx == 0 else knw_ref
                y = _head_rms(y, seg_ref[...]) * nw[...]
            y = _rope(y, cos_ref[:, :wdt], sin_ref[:, :wdt])
            if idx == 0:
                y = y * Q_SCALE
        if idx == 2:
            o_ref[0, 0] = y.T.astype(o_ref.dtype)
        else:
            o_ref[0] = y.astype(o_ref.dtype)
        off += wdt


def _premix_call(x_all, mod, norm_w, w_in, cos, sin, qnw, knw, seg, widths, dtypes, qk_norm,
                 n_batch, ctx_tiles, tm):
    bsz, t_all, _ = x_all.shape
    nt = t_all // tm
    n_in = w_in.shape[1]
    const = lambda b, t: (0, 0)
    tok = lambda b, t: (b, t, 0)
    out_specs = [pl.BlockSpec((1, tm, w), tok) for w in widths]
    out_shape = [jax.ShapeDtypeStruct((bsz, t_all, w), d) for w, d in zip(widths, dtypes)]
    out_specs[2] = pl.BlockSpec((1, 1, widths[2], tm), lambda b, t: (b, t, 0, 0))
    out_shape[2] = jax.ShapeDtypeStruct((bsz, nt, widths[2], tm), dtypes[2])
    return pl.pallas_call(
        functools.partial(_premix_kernel, widths=widths, qk_norm=qk_norm),
        grid=(bsz, nt),
        in_specs=[pl.BlockSpec((1, tm, D_MODEL), tok),
                  _mod_spec(n_batch, ctx_tiles, 0),
                  pl.BlockSpec((1, D_MODEL), const),
                  pl.BlockSpec((D_MODEL, n_in), const),
                  pl.BlockSpec((tm, MIX_W), lambda b, t: (t, 0)),
                  pl.BlockSpec((tm, MIX_W), lambda b, t: (t, 0)),
                  pl.BlockSpec((1, MIX_W), const),
                  pl.BlockSpec((1, KV_DUP_W), const),
                  pl.BlockSpec((MIX_W, MIX_W), const)],
        out_specs=out_specs,
        out_shape=out_shape,
        compiler_params=_cparams(("parallel", "arbitrary")),
        name="premix_qknorm" if qk_norm else "premix",
    )(x_all, mod, norm_w, w_in, cos, sin, qnw, knw, seg)


NEG_BIG = -1e30
SCORE_LOOKAHEAD = 2


def _masked_q_pairs(q_ref):
    tq = q_ref.shape[1]
    lane = lax.broadcasted_iota(jnp.int32, (tq, 2 * HEAD_DIM), 1)
    pairs = []
    for j in range(N_HEADS // 2):
        slab = q_ref[0, :, 128 * j:128 * j + 128]
        zero = jnp.zeros_like(slab)
        pairs.append(jnp.concatenate([jnp.where(lane < HEAD_DIM, slab, zero),
                                      jnp.where(lane >= HEAD_DIM, slab, zero)], axis=0))
    return pairs


def _fold_rows_max(s):
    m = s[0:8]
    for i in range(1, s.shape[0] // 8):
        m = jnp.maximum(m, s[8 * i:8 * i + 8])
    return m


def _attend(q_ref, o_ref, blocks, sinks):
    tq = q_ref.shape[1]
    n_pairs = N_HEADS // 2
    ppg = n_pairs // KV_HEADS
    q_pairs = _masked_q_pairs(q_ref)
    row = lax.broadcasted_iota(jnp.int32, (HEAD_DIM + ONES_ROWS, 2 * tq), 0)
    if sinks is None:
        m = [jnp.full((1, 2 * tq), NEG_BIG, F32) for _ in range(n_pairs)]
        acc = [jnp.zeros((HEAD_DIM + ONES_ROWS, 2 * tq), F32) for _ in range(n_pairs)]
    else:
        m = list(sinks)
        acc = [jnp.where(row >= HEAD_DIM, 1.0, 0.0) for _ in range(n_pairs)]
    def scores(blk):
        k_of, _, mask = blk
        out = []
        for g in range(KV_HEADS):
            kc = k_of(g)
            for r in range(ppg):
                sc = _dot_nt(kc, q_pairs[g * ppg + r])
                out.append(sc if mask is None else jnp.where(mask, sc, NEG_BIG))
        return out

    ahead = [scores(b) for b in blocks[:SCORE_LOOKAHEAD]]
    for i, (_, vt_of, _) in enumerate(blocks):
        s = ahead.pop(0)
        if i + SCORE_LOOKAHEAD < len(blocks):
            ahead.append(scores(blocks[i + SCORE_LOOKAHEAD]))
        vts = []
        for g in range(KV_HEADS):
            vt = vt_of(g)
            vts.append(jnp.concatenate([vt, jnp.ones((ONES_ROWS, vt.shape[1]), BF16)], axis=0))
        p, alpha = [], []
        for j in range(n_pairs):
            m_new = jnp.maximum(m[j], jnp.max(_fold_rows_max(s[j]), axis=0, keepdims=True))
            alpha.append(jnp.exp2(m[j] - m_new))
            p.append(jnp.exp2(s[j] - m_new).astype(BF16))
            m[j] = m_new
        for j in range(n_pairs):
            acc[j] = alpha[j] * acc[j] + jnp.dot(vts[j // ppg], p[j], preferred_element_type=F32)
    outs = []
    for a in acc:
        o = a[:HEAD_DIM] / a[HEAD_DIM:HEAD_DIM + 1]
        outs += [o[:, :tq], o[:, tq:]]
    o_ref[0] = jnp.concatenate(outs, axis=0).T.astype(o_ref.dtype)


def _gqa_kernel(q_ref, k_ref, vt_ref, o_ref, *, ctx_chunks, ctx_tiles):
    kc_rows = vt_ref.shape[3]

    def blocks(n_chunks):
        return [(lambda g, j=j: k_ref[0, j * kc_rows:(j + 1) * kc_rows, 128 * g:128 * g + 128],
                 lambda g, j=j: vt_ref[0, j, HEAD_DIM * g:HEAD_DIM * (g + 1), :],
                 None) for j in range(n_chunks)]

    t = pl.program_id(1)

    @pl.when(t < ctx_tiles)
    def _():
        _attend(q_ref, o_ref, blocks(ctx_chunks), None)

    @pl.when(t >= ctx_tiles)
    def _():
        _attend(q_ref, o_ref, blocks(vt_ref.shape[1]), None)


def _gqa_call(q, kdup, vt, ctx_len):
    bsz, t_all, _ = q.shape
    _, n_chunks, vw, kc_rows = vt.shape
    tq = GQA_TQ if ctx_len % GQA_TQ == 0 else ATT_TQ
    return pl.pallas_call(
        functools.partial(_gqa_kernel, ctx_chunks=ctx_len // kc_rows, ctx_tiles=ctx_len // tq),
        grid=(bsz, t_all // tq),
        in_specs=[pl.BlockSpec((1, tq, MIX_W), lambda b, t: (b, t, 0)),
                  pl.BlockSpec((1, t_all, KV_DUP_W), lambda b, t: (b, 0, 0)),
                  pl.BlockSpec((1, n_chunks, vw, kc_rows), lambda b, t: (b, 0, 0, 0))],
        out_specs=pl.BlockSpec((1, tq, MIX_W), lambda b, t: (b, t, 0)),
        out_shape=jax.ShapeDtypeStruct((bsz, t_all, MIX_W), BF16),
        compiler_params=_cparams(("parallel", "arbitrary")),
        name="global_gqa",
    )(q, kdup, vt)


def _swa_kernel(sink_ref, q_ref, kc_ref, vtc_ref, kp_ref, kq_ref, kn_ref, vtp_ref, vtq_ref, vtn_ref,
                o_ref, *, ctx_tiles, tile_off, n_tiles_all):
    tq = q_ref.shape[1]
    t = pl.program_id(1) + tile_off
    lane2 = lax.broadcasted_iota(jnp.int32, (1, 2 * tq), 1)
    sinks = [jnp.where(lane2 < tq, sink_ref[2 * j] * LOG2E, sink_ref[2 * j + 1] * LOG2E)
             for j in range(N_HEADS // 2)]
    kc_rows = vtc_ref.shape[3]
    ctx_blocks = [(lambda g, j=j: kc_ref[0, j * kc_rows:(j + 1) * kc_rows, 128 * g:128 * g + 128],
                   lambda g, j=j: vtc_ref[0, j, HEAD_DIM * g:HEAD_DIM * (g + 1), :],
                   None) for j in range(vtc_ref.shape[1])]

    def near(k_ref, vt_ref, mask):
        return (lambda g: k_ref[0, :, 128 * g:128 * g + 128],
                lambda g: vt_ref[0, 0, HEAD_DIM * g:HEAD_DIM * (g + 1), :], mask)

    @pl.when(t < ctx_tiles)
    def _():
        _attend(q_ref, o_ref, ctx_blocks, sinks)

    @pl.when(t >= ctx_tiles)
    def _():
        key_i = lax.broadcasted_iota(jnp.int32, (tq, 2 * tq), 0)
        qry_i = lax.broadcasted_iota(jnp.int32, (tq, 2 * tq), 1) & (tq - 1)
        lim_prev = jnp.where(t > ctx_tiles, 0, tq)
        lim_next = jnp.where(t < n_tiles_all - 1, 0, tq)
        _attend(q_ref, o_ref,
                ctx_blocks + [near(kp_ref, vtp_ref, key_i - qry_i >= lim_prev),
                              near(kq_ref, vtq_ref, None),
                              near(kn_ref, vtn_ref, qry_i - key_i >= lim_next)], sinks)


def _swa_call(sink, q, kdup, vt, ctx_len, lat_only):
    bsz, t_all, _ = q.shape
    _, _, vw, kc_rows = vt.shape
    tq = ATT_TQ
    per = kc_rows // tq
    ctx_tiles = ctx_len // tq
    n_all = t_all // tq
    off = ctx_tiles if lat_only else 0
    cur = lambda b, t: t + off
    prv = lambda b, t: jnp.maximum(t + off - 1, ctx_tiles)
    nxt = lambda b, t: jnp.minimum(t + off + 1, n_all - 1)
    k_spec = lambda tile: pl.BlockSpec((1, tq, KV_DUP_W), lambda b, t: (b, tile(b, t), 0))
    vt_spec = lambda tile: pl.BlockSpec((1, 1, vw, tq),
                                        lambda b, t: (b, tile(b, t) // per, 0, tile(b, t) % per))
    return pl.pallas_call(
        functools.partial(_swa_kernel, ctx_tiles=ctx_tiles, tile_off=off, n_tiles_all=n_all),
        grid=(bsz, n_all - off),
        in_specs=[pl.BlockSpec(memory_space=pltpu.SMEM),
                  pl.BlockSpec((1, tq, MIX_W), lambda b, t: (b, t + off, 0)),
                  pl.BlockSpec((1, ctx_len, KV_DUP_W), lambda b, t: (b, 0, 0)),
                  pl.BlockSpec((1, ctx_len // kc_rows, vw, kc_rows), lambda b, t: (b, 0, 0, 0)),
                  k_spec(prv), k_spec(cur), k_spec(nxt), vt_spec(prv), vt_spec(cur), vt_spec(nxt)],
        out_specs=pl.BlockSpec((1, tq, MIX_W), lambda b, t: (b, t, 0)),
        out_shape=jax.ShapeDtypeStruct((bsz, t_all - off * tq, MIX_W), BF16),
        compiler_params=_cparams(("parallel", "arbitrary")),
        name="window_gqa",
    )(sink, q, kdup, vt, kdup, kdup, kdup, vt, vt, vt)


def _segment_flags(t, ctx_tiles, n_tiles):
    first = (t == 0) | (t == ctx_tiles)
    last = (t == ctx_tiles - 1) | (t == n_tiles - 1)
    return first, last


def _rwkv_prep_kernel(f_ref, fp_ref, fn_ref, mup_ref, mun_ref, w0_ref, w2_ref, a0_ref, a2_ref, g2_ref,
                      kk_w_ref, ka_ref, rk_ref, seg_ref,
                      r_ref, v_ref, kk_ref, lw_ref, bb_ref, kd_ref, bonus_ref, g_ref,
                      *, ctx_tiles, n_tiles):
    first, last = _segment_flags(pl.program_id(1), ctx_tiles, n_tiles)
    cur = f_ref[0].astype(F32)
    prev8 = jnp.where(first, 0.0, fp_ref[0].astype(F32)[HALO_BF16 - HALO:])
    next8 = jnp.where(last, 0.0, fn_ref[0].astype(F32)[:HALO])
    prv = _shift_rows(cur, prev8, next8, 1)
    nxt = _shift_rows(cur, prev8, next8, -1)
    f = cur + mup_ref[...] * (prv - cur) + mun_ref[...] * (nxt - cur)
    r = f[:, 0:MIX_W]
    k = f[:, MIX_W:2 * MIX_W]
    v = f[:, 2 * MIX_W:3 * MIX_W]
    wd = f[:, 3 * MIX_W:3 * MIX_W + B_LORA_W]
    ad = f[:, 3 * MIX_W + B_LORA_W:3 * MIX_W + 2 * B_LORA_W]
    gd = f[:, 3 * MIX_W + 2 * B_LORA_W:]
    wlog = w0_ref[...] + _dot(jnp.tanh(wd), w2_ref[...])
    lw_ref[0] = -math.exp(-0.5) * _sigmoid(wlog)
    a = _sigmoid(a0_ref[...] + _dot(ad, a2_ref[...]))
    kk = k * kk_w_ref[...]
    kk = kk * lax.rsqrt(_dot(kk * kk, seg_ref[...]) + 1e-12)
    ksum = jnp.zeros_like(k)
    for d in range(2):
        a_d = a[:, d * MIX_W:(d + 1) * MIX_W]
        k_d = k * (1.0 + (a_d - 1.0) * ka_ref[...])
        bb_ref[0, :, d * MIX_W:(d + 1) * MIX_W] = (a_d * kk).astype(bb_ref.dtype)
        kd_ref[0, :, d * MIX_W:(d + 1) * MIX_W] = k_d.astype(kd_ref.dtype)
        ksum = ksum + k_d
    r_ref[0] = r.astype(r_ref.dtype)
    v_ref[0] = v.astype(v_ref.dtype)
    kk_ref[0] = kk.astype(kk_ref.dtype)
    bonus_ref[0] = (_dot_exact_rhs(r * ksum * rk_ref[...], seg_ref[...]) * v).astype(bonus_ref.dtype)
    g_ref[0] = _dot(_sigmoid(gd), g2_ref[...]).astype(g_ref.dtype)


def _rwkv_prep_call(f, p, ctx_len, tm):
    bsz, t_all, _ = f.shape
    nt = t_all // tm
    ctx_tiles = ctx_len // tm
    hb = HALO_BF16
    n_halo = t_all // hb
    const = lambda b, t: (0, 0)
    tok = lambda b, t: (b, t, 0)
    wide = jax.ShapeDtypeStruct((bsz, t_all, 2 * MIX_W), BF16)
    one = jax.ShapeDtypeStruct((bsz, t_all, MIX_W), BF16)
    logs = jax.ShapeDtypeStruct((bsz, t_all, 2 * MIX_W), F32)
    row = lambda w: pl.BlockSpec((1, w), const)
    return pl.pallas_call(
        functools.partial(_rwkv_prep_kernel, ctx_tiles=ctx_tiles, n_tiles=nt),
        grid=(bsz, nt),
        in_specs=[pl.BlockSpec((1, tm, B_IN), tok),
                  pl.BlockSpec((1, hb, B_IN), lambda b, t: (b, jnp.maximum(t * (tm // hb) - 1, 0), 0)),
                  pl.BlockSpec((1, hb, B_IN), lambda b, t: (b, jnp.minimum((t + 1) * (tm // hb), n_halo - 1), 0)),
                  row(B_IN), row(B_IN),
                  row(2 * MIX_W), pl.BlockSpec((B_LORA_W, 2 * MIX_W), const),
                  row(2 * MIX_W), pl.BlockSpec((B_LORA_W, 2 * MIX_W), const),
                  pl.BlockSpec((B_LORA_W, MIX_W), const),
                  row(MIX_W), row(MIX_W), row(MIX_W),
                  pl.BlockSpec((MIX_W, MIX_W), const)],
        out_specs=[pl.BlockSpec((1, tm, MIX_W), tok)] * 3 + [pl.BlockSpec((1, tm, 2 * MIX_W), tok)] * 3
                  + [pl.BlockSpec((1, tm, MIX_W), tok)] * 2,
        out_shape=[one, one, one, logs, wide, wide, one, one],
        compiler_params=_cparams(("parallel", "arbitrary")),
        name="rwkv_prep",
    )(f, f, f, p["mu_prev"], p["mu_next"], p["w0"], p["w2"], p["a0"], p["a2"], p["g2"],
      p["k_k"], p["k_a"], p["r_k"], p["seg"])


def _scan_chunk(s, d, n_ctx, n_all):
    bwd = jnp.where(s < n_ctx, n_ctx - 1 - s, n_all - 1 + n_ctx - s)
    return jnp.where(d == 0, s, bwd)


def _block_diag2(x):
    lane = lax.broadcasted_iota(jnp.int32, x.shape, 1)
    zero = jnp.zeros_like(x)
    return jnp.concatenate([jnp.where(lane < HEAD_DIM, x, zero), jnp.where(lane < HEAD_DIM, zero, x)], axis=0)


def _rwkv_scan_kernel(rf_ref, vf_ref, kkf_ref, lwf_ref, bbf_ref, kdf_ref,
                      rb_ref, vb_ref, kkb_ref, lwb_ref, bbb_ref, kdb_ref,
                      yf_ref, yb_ref, state_ref):
    c = rf_ref.shape[1]

    @pl.when(pl.program_id(1) == 0)
    def _():
        state_ref[...] = jnp.zeros_like(state_ref)

    ri = lax.broadcasted_iota(jnp.int32, (c, c), 0)
    ci = lax.broadcasted_iota(jnp.int32, (c, c), 1)
    ri2 = lax.broadcasted_iota(jnp.int32, (c, 2 * c), 0)
    ci2 = lax.broadcasted_iota(jnp.int32, (c, 2 * c), 1) & (c - 1)
    eye2 = jnp.where(ri2 == ci2, 1.0, 0.0)

    units = []
    dir_refs = ((rf_ref, vf_ref, kkf_ref, lwf_ref, bbf_ref, kdf_ref),
                (rb_ref, vb_ref, kkb_ref, lwb_ref, bbb_ref, kdb_ref))
    for d, bi in [(d, bi) for d in range(2) for bi in range(rf_ref.shape[0])]:
        r_ref, v_ref, kk_ref, lw_ref, bb_ref, kd_ref = dir_refs[d]
        incl = (ri - ci if d == 0 else ci - ri) >= 0
        lw = lw_ref[bi]
        cum = _dot_exact_lhs(jnp.where(incl, 1.0, 0.0).astype(BF16), lw)
        e_cum = jnp.exp(cum)
        e_inv = jnp.exp(-cum)
        e_tot = jnp.exp(jnp.sum(lw, axis=0, keepdims=True))
        qt = r_ref[bi] * e_cum
        kkt = kk_ref[bi] * jnp.exp(cum - lw)
        kh = kd_ref[bi] * e_inv
        bh = bb_ref[bi] * e_inv
        k_end = kh * e_tot
        b_end = -(bh * e_tot)
        v = v_ref[bi].astype(F32)
        incl2 = (ri2 - ci2 if d == 0 else ci2 - ri2) >= 0
        strict2 = (ri2 - ci2 if d == 0 else ci2 - ri2) > 0
        for j in range(N_HEADS // 2):
            sl = slice(2 * HEAD_DIM * j, 2 * HEAD_DIM * (j + 1))
            units.append(dict(
                d=d, bi=bi, j=j, sl=sl, incl=incl2, strict=strict2, v=v[:, sl], e_tot=e_tot[:, sl],
                lhs=jnp.concatenate([qt[:, sl], kkt[:, sl]], axis=0).astype(BF16),
                rhs=jnp.concatenate([_block_diag2(kh[:, sl].astype(BF16)),
                                     _block_diag2(bh[:, sl].astype(BF16))], axis=0),
                x_end=jnp.concatenate([k_end[:, sl], b_end[:, sl]], axis=0).astype(BF16)))

    c2 = 2 * c
    for u in units:
        a = _dot_nt(u["lhs"], u["rhs"])
        u["a_v"] = jnp.concatenate([jnp.where(u["incl"], a[:c, :c2], 0.0),
                                    jnp.where(u["strict"], a[c:, :c2], 0.0)], axis=0).astype(BF16)
        u["a_rb"] = jnp.where(u["incl"], a[:c, c2:], 0.0).astype(BF16)
        u["p"] = -jnp.where(u["strict"], a[c:, c2:], 0.0)
    for u in units:
        pb = u["p"].astype(BF16)
        u["t"] = eye2 + u["p"]
        u["p"] = jnp.dot(pb, _block_diag2(pb), preferred_element_type=F32)
    m = 2
    while 2 * m < c:
        for u in units:
            pb = u["p"].astype(BF16)
            both = jnp.dot(jnp.concatenate([u["t"].astype(BF16), pb], axis=0), _block_diag2(pb),
                           preferred_element_type=F32)
            u["t"] = u["t"] + both[:c]
            u["p"] = both[c:]
        m *= 2
    for u in units:
        u["t"] = (u["t"] + _dot(u["t"], _block_diag2(u["p"].astype(BF16)))).astype(BF16)
        u["av"] = _dot(u["a_v"], _block_diag2(u["v"].astype(BF16)))
    for u in units:
        u["st"] = state_ref[u["d"], u["bi"], u["j"]]
        u["x1"] = _dot_nt(u["lhs"], u["st"]) + u["av"]
    for u in units:
        u["u"] = _dot(u["t"], _block_diag2(u["x1"][c:].astype(BF16)))
    rr = lax.broadcasted_iota(jnp.int32, (c2, c2), 0)
    cc = lax.broadcasted_iota(jnp.int32, (c2, c2), 1)
    same_head = (rr // HEAD_DIM) == (cc // HEAD_DIM)
    for u in units:
        y = u["x1"][:c] - _dot(u["a_rb"], _block_diag2(u["u"].astype(BF16)))
        (yf_ref, yb_ref)[u["d"]][u["bi"], :, u["sl"]] = y.astype(yf_ref.dtype)
        wv = jnp.concatenate([u["v"], u["u"]], axis=0)
        upd = jnp.where(same_head, _dot(wv.T, u["x_end"]), 0.0)
        state_ref[u["d"], u["bi"], u["j"]] = u["st"] * u["e_tot"] + upd


def _rwkv_scan_call(r, v, kk, lw, bb, kd, ctx_len):
    bsz, t_all, _ = r.shape
    c = RWKV_CHUNK
    nb = RWKV_SAMPLES_PER_STEP if bsz % RWKV_SAMPLES_PER_STEP == 0 else 1
    n_all = t_all // c
    n_ctx = ctx_len // c
    blk = lambda im: pl.BlockSpec((nb, c, MIX_W), im)
    fwd = lambda b, s: (b, s, 0)
    bwd = lambda b, s: (b, _scan_chunk(s, 1, n_ctx, n_all), 0)
    bwd1 = lambda b, s: (b, _scan_chunk(s, 1, n_ctx, n_all), 1)
    out = jax.ShapeDtypeStruct((bsz, t_all, MIX_W), BF16)
    return pl.pallas_call(
        _rwkv_scan_kernel,
        grid=(bsz // nb, n_all),
        in_specs=[blk(fwd)] * 6 + [blk(bwd)] * 3 + [blk(bwd1)] * 3,
        out_specs=[blk(fwd), blk(bwd)],
        out_shape=[out, out],
        scratch_shapes=[pltpu.VMEM((2, nb, N_HEADS // 2, 2 * HEAD_DIM, 2 * HEAD_DIM), F32)],
        compiler_params=_cparams(("parallel", "arbitrary")),
        name="rwkv_scan",
    )(r, v, kk, lw, bb, kd, r, v, kk, lw, bb, kd)


def _conv_kernel(x_ref, xp_ref, xn_ref, w_ref, b_ref, o_ref, *, ctx_tiles, n_tiles):
    first, last = _segment_flags(pl.program_id(1), ctx_tiles, n_tiles)
    cur = x_ref[0].astype(F32)
    prev8 = jnp.where(first, 0.0, xp_ref[0].astype(F32)[HALO_BF16 - HALO:])
    next8 = jnp.where(last, 0.0, xn_ref[0].astype(F32)[:HALO])
    acc = b_ref[...] + w_ref[D_CONV // 2:D_CONV // 2 + 1, :] * cur
    for kpos in range(D_CONV):
        s = D_CONV // 2 - kpos
        if s != 0:
            acc = acc + w_ref[kpos:kpos + 1, :] * _shift_rows(cur, prev8, next8, s)
    o_ref[0] = (acc * _sigmoid(acc)).astype(o_ref.dtype)


def _conv_call(xbc, conv_w, conv_b, ctx_len, tm):
    bsz, t_all, w = xbc.shape
    nt = t_all // tm
    hb = HALO_BF16
    n_halo = t_all // hb
    return pl.pallas_call(
        functools.partial(_conv_kernel, ctx_tiles=ctx_len // tm, n_tiles=nt),
        grid=(bsz, nt),
        in_specs=[pl.BlockSpec((1, tm, w), lambda b, t: (b, t, 0)),
                  pl.BlockSpec((1, hb, w), lambda b, t: (b, jnp.maximum(t * (tm // hb) - 1, 0), 0)),
                  pl.BlockSpec((1, hb, w), lambda b, t: (b, jnp.minimum((t + 1) * (tm // hb), n_halo - 1), 0)),
                  pl.BlockSpec((D_CONV, w), lambda b, t: (0, 0)),
                  pl.BlockSpec((1, w), lambda b, t: (0, 0))],
        out_specs=pl.BlockSpec((1, tm, w), lambda b, t: (b, t, 0)),
        out_shape=jax.ShapeDtypeStruct((bsz, t_all, w), BF16),
        compiler_params=_cparams(("parallel", "arbitrary")),
        name="ssd_conv",
    )(xbc, xbc, xbc, conv_w, conv_b)


def _ssd_scan_kernel(uf_ref, dtf_ref, ub_ref, dtb_ref, bias_ref, a_ref, sel_ref, exp_ref,
                     yf_ref, yb_ref, state_ref):
    c = uf_ref.shape[1]

    @pl.when(pl.program_id(1) == 0)
    def _():
        state_ref[...] = jnp.zeros_like(state_ref)

    ri = lax.broadcasted_iota(jnp.int32, (c, c), 0)
    ci = lax.broadcasted_iota(jnp.int32, (c, c), 1)
    expand = exp_ref[...]
    gw = D_STATE
    gl = MIX_W // D_GROUPS
    hpg = N_HEADS // D_GROUPS
    dirs = []
    for d, bi in [(d, bi) for d in range(2) for bi in range(uf_ref.shape[0])]:
        u_ref, dt_ref = ((uf_ref, dtf_ref), (ub_ref, dtb_ref))[d]
        incl = (ri - ci if d == 0 else ci - ri) >= 0
        incl_m = jnp.where(incl, 1.0, 0.0).astype(BF16)
        dt_all = _softplus(dt_ref[bi] + bias_ref[...])
        dt_d = _dot_exact_rhs(dt_all, sel_ref[d])
        dta_d = _dot_exact_rhs(dt_all * a_ref[...], sel_ref[d])
        cum_d = _dot_exact_lhs(incl_m, dta_d)
        dta_x = _dot_exact_rhs(dta_d, expand)
        dirs.append(dict(d=d, bi=bi, u_ref=u_ref, incl=incl, cum_d=cum_d, cum_t=cum_d.T, dt_t=dt_d.T,
                         cum_x=_dot_exact_lhs(incl_m, dta_x), dt_x=_dot_exact_rhs(dt_d, expand),
                         tot_x=jnp.sum(dta_x, axis=0, keepdims=True), xs=u_ref[bi, :, 0:MIX_W], ys=[]))

    def b_mat(p, g):
        return p["u_ref"][p["bi"], :, MIX_W + gw * g:MIX_W + gw * (g + 1)]

    def c_mat(p, g):
        return p["u_ref"][p["bi"], :, MIX_W + gw * D_GROUPS + gw * g:MIX_W + gw * D_GROUPS + gw * (g + 1)]

    def head_matrix(p, h):
        seg = p["cum_d"][:, h:h + 1] - p["cum_t"][h:h + 1, :]
        lmat = jnp.exp(jnp.where(p["incl"], seg, -jnp.inf))
        return (p["cb"] * lmat * p["dt_t"][h:h + 1, :]).astype(BF16)

    for g in range(D_GROUPS):
        for p in dirs:
            p["cb"] = _dot_nt(c_mat(p, g), b_mat(p, g))
        for r in range(0, hpg, 2):
            h = g * hpg + r
            for p in dirs:
                mm = jnp.concatenate([head_matrix(p, h), head_matrix(p, h + 1)], axis=1)
                x_pair = p["xs"][:, HEAD_DIM * h:HEAD_DIM * (h + 2)]
                p["ys"].append(jnp.dot(mm, _block_diag2(x_pair), preferred_element_type=F32))

    for p in dirs:
        d, bi = p["d"], p["bi"]
        wj = jnp.exp(p["tot_x"] - p["cum_x"]) * p["dt_x"]
        xw = p["xs"] * wj
        e_tot = jnp.exp(p["tot_x"])
        offs = []
        for g in range(D_GROUPS):
            st = state_ref[d, bi, :, gl * g:gl * (g + 1)]
            offs.append(_dot(c_mat(p, g), st))
            bm_t = b_mat(p, g).astype(F32).T
            state_ref[d, bi, :, gl * g:gl * (g + 1)] = (st * e_tot[:, gl * g:gl * (g + 1)]
                                                        + _dot(bm_t, xw[:, gl * g:gl * (g + 1)]))
        (yf_ref, yb_ref)[d][bi] = (jnp.concatenate(p["ys"], axis=1)
                                   + jnp.concatenate(offs, axis=1) * jnp.exp(p["cum_x"])).astype(yf_ref.dtype)


def _ssd_scan_call(u, dt_raw, dtb, a_row, sel, expand, ctx_len):
    bsz, t_all, w = u.shape
    c = SSD_CHUNK
    nb = SSD_SAMPLES_PER_STEP if bsz % SSD_SAMPLES_PER_STEP == 0 else 1
    n_all = t_all // c
    n_ctx = ctx_len // c
    fwd = lambda b, s: (b, s, 0)
    bwd = lambda b, s: (b, _scan_chunk(s, 1, n_ctx, n_all), 0)
    const = lambda b, s: (0, 0)
    out = jax.ShapeDtypeStruct((bsz, t_all, MIX_W), BF16)
    return pl.pallas_call(
        _ssd_scan_kernel,
        grid=(bsz // nb, n_all),
        in_specs=[pl.BlockSpec((nb, c, w), fwd), pl.BlockSpec((nb, c, DT_PAD), fwd),
                  pl.BlockSpec((nb, c, w), bwd), pl.BlockSpec((nb, c, DT_PAD), bwd),
                  pl.BlockSpec((1, DT_PAD), const),
                  pl.BlockSpec((1, DT_PAD), const),
                  pl.BlockSpec((2, DT_PAD, DT_PAD), lambda b, s: (0, 0, 0)),
                  pl.BlockSpec((DT_PAD, MIX_W), const)],
        out_specs=[pl.BlockSpec((nb, c, MIX_W), fwd), pl.BlockSpec((nb, c, MIX_W), bwd)],
        out_shape=[out, out],
        scratch_shapes=[pltpu.VMEM((2, nb, D_STATE, MIX_W), F32)],
        compiler_params=_cparams(("parallel", "arbitrary")),
        name="ssd_scan",
    )(u, dt_raw, u, dt_raw, dtb, a_row, sel, expand)


def _rwkv_mix(yf_ref, yb_ref, bonus_ref, g_ref, lnw_ref, lnb_ref, seg_ref):
    y = yf_ref[0].astype(F32) + yb_ref[0].astype(F32)
    inv = 1.0 / HEAD_DIM
    mu = _dot_exact_rhs(y, seg_ref[...]) * inv
    yc = y - mu
    var = _dot(yc * yc, seg_ref[...]) * inv
    yn = yc * lax.rsqrt(var + B_GN_EPS) * lnw_ref[...] + lnb_ref[...]
    return (yn + bonus_ref[0]) * g_ref[0]


def _ssd_mix(yf_ref, yb_ref, u_ref, z_ref, dsk_ref, nw_ref):
    z = z_ref[0].astype(F32)
    y = ((yf_ref[0].astype(F32) + yb_ref[0].astype(F32) + dsk_ref[...] * u_ref[0].astype(F32))
         * (z * _sigmoid(z)))
    gl = MIX_W // D_GROUPS
    parts = []
    for g in range(D_GROUPS):
        yg = y[:, gl * g:gl * (g + 1)]
        parts.append(yg * lax.rsqrt(jnp.mean(yg * yg, axis=-1, keepdims=True) + NORM_EPS))
    return jnp.concatenate(parts, axis=1) * nw_ref[...]


def _tail_kernel(x_ref, mod_ref, att_ref, *refs, kind, final):
    n_mix = 7 if kind == "rwkv" else 6
    mix = (_rwkv_mix if kind == "rwkv" else _ssd_mix)(*refs[:n_mix])
    wo_ref, nw_ref, w1_ref, w2_ref, fw_ref, o_ref = refs[n_mix:]
    out = (jnp.dot(att_ref[0], wo_ref[0:MIX_W, :], preferred_element_type=F32)
           + jnp.dot(mix.astype(BF16), wo_ref[MIX_W:, :], preferred_element_type=F32))
    x = x_ref[0] + mod_ref[0, 2:3, :] * out
    ms = jnp.mean(x * x, axis=-1, keepdims=True)
    h = x * lax.rsqrt(ms + NORM_EPS) * nw_ref[...]
    hb = (h * (1.0 + mod_ref[0, 4:5, :]) + mod_ref[0, 3:4, :]).astype(BF16)
    acc = jnp.zeros_like(x)
    for j in range(MLP_HIDDEN // D_MODEL):
        a = jnp.maximum(jnp.dot(hb, w1_ref[:, D_MODEL * j:D_MODEL * (j + 1)], preferred_element_type=F32), 0.0)
        acc = acc + jnp.dot((a * a).astype(BF16), w2_ref[D_MODEL * j:D_MODEL * (j + 1), :],
                            preferred_element_type=F32)
    x = x + mod_ref[0, 5:6, :] * acc
    if final:
        x = x * lax.rsqrt(jnp.mean(x * x, axis=-1, keepdims=True) + NORM_EPS) * fw_ref[...]
    o_ref[0] = x


def _tail_call(kind, x_all, mod, att, extras, rows, w_out, norm_w, w1, w2, final_w,
               n_batch, ctx_len, tm, lat_only, final):
    bsz, t_all, _ = x_all.shape
    ctx_tiles = ctx_len // tm
    off = ctx_tiles if lat_only else 0
    nt = t_all // tm - off
    tok = lambda b, t: (b, t + off, 0)
    att_off = 0 if att.shape[1] == t_all else -off
    const = lambda b, t: (0, 0)
    weights = lambda shape: pl.BlockSpec(shape, const, pipeline_mode=pl.Buffered(1))
    in_specs = [pl.BlockSpec((1, tm, D_MODEL), tok),
                _mod_spec(n_batch, ctx_tiles, off),
                pl.BlockSpec((1, tm, MIX_W), lambda b, t: (b, t + off + att_off, 0))]
    in_specs += [pl.BlockSpec((1, tm, MIX_W), tok) for _ in extras]
    in_specs += [pl.BlockSpec(r.shape, const) for r in rows]
    in_specs += [weights((D_MODEL, D_MODEL)),
                 pl.BlockSpec((1, D_MODEL), const),
                 weights((D_MODEL, MLP_HIDDEN)),
                 weights((MLP_HIDDEN, D_MODEL)),
                 pl.BlockSpec((1, D_MODEL), const)]
    return pl.pallas_call(
        functools.partial(_tail_kernel, kind=kind, final=final),
        grid=(bsz, nt),
        in_specs=in_specs,
        out_specs=pl.BlockSpec((1, tm, D_MODEL), lambda b, t: (b, t, 0)),
        out_shape=jax.ShapeDtypeStruct((bsz, nt * tm, D_MODEL), F32),
        compiler_params=_cparams(("parallel", "arbitrary")),
        name="tail_" + kind + ("_final" if final else ""),
    )(x_all, mod, att, *extras, *rows, w_out, norm_w, w1, w2, final_w)


def _rope_tables(n_lat, ctx_len):
    rows = n_lat // GRID_W
    row = jnp.repeat(jnp.arange(rows, dtype=F32), GRID_W)
    col = jnp.tile(jnp.arange(GRID_W, dtype=F32), rows)
    n_freq = HEAD_DIM // 4
    inv_freq = ROPE_THETA ** (-jnp.arange(n_freq, dtype=F32) / n_freq)
    ang_r = row[:, None] * inv_freq
    ang_c = col[:, None] * inv_freq
    cr, sr, cc, sc = jnp.cos(ang_r), jnp.sin(ang_r), jnp.cos(ang_c), jnp.sin(ang_c)
    cos = jnp.concatenate([cr, cr, cc, cc], axis=-1)
    sin = jnp.concatenate([-sr, sr, -sc, sc], axis=-1)
    cos = jnp.concatenate([jnp.ones((ctx_len, HEAD_DIM), F32), cos], axis=0)
    sin = jnp.concatenate([jnp.zeros((ctx_len, HEAD_DIM), F32), sin], axis=0)
    return jnp.tile(cos, (1, N_HEADS)), jnp.tile(sin, (1, N_HEADS))


def _dup_kv_cols(w):
    k0, k1 = w[:, :HEAD_DIM], w[:, HEAD_DIM:]
    return jnp.concatenate([k0, k0, k1, k1], axis=1)


def _seg_matrix():
    return jnp.asarray(np.kron(np.eye(N_HEADS), np.ones((HEAD_DIM, HEAD_DIM))), BF16)


def _tile_rows(ctx_len):
    return 256 if ctx_len % 256 == 0 else 128


def _forward(x, c, ctx, c_ctx, w_mod, b_mod, norm_mix, norm_mlp, w_out, mlp_w1, mlp_w2, final_norm,
             ab_w_in, a_q_norm, a_k_norm, b_mu_prev, b_mu_next, b_w0, b_w2, b_a0, b_a2, b_g2,
             b_k_k, b_k_a, b_r_k, b_ln_w, b_ln_b,
             cd_w_in, c_sink, d_conv_w, d_conv_b, d_dt_bias, d_A_log, d_D, d_norm_w):
    bsz, n_lat, _ = x.shape
    ctx_len = ctx.shape[1]
    tm = _tile_rows(ctx_len)
    ctx_tiles = ctx_len // tm
    mod_rows = -(-(bsz + 1) // 8) * 8
    c_all = jnp.concatenate([c, c_ctx[None, :], jnp.zeros((mod_rows - bsz - 1, D_MODEL), F32)], axis=0)
    mods = _mod_call(c_all, w_mod, b_mod).reshape(DEPTH, mod_rows, 6, D_MODEL)
    cos, sin = _rope_tables(n_lat, ctx_len)
    seg = _seg_matrix()
    row = lambda a: a.reshape(1, -1).astype(F32)
    x_all = jnp.concatenate([ctx, x], axis=1)
    head_dup = lambda w, n: jnp.tile(w.astype(F32), n).reshape(1, -1)

    for i in range(DEPTH):
        j = i // 2
        last = i == DEPTH - 1
        mod = mods[i]
        if i % 2 == 0:
            w = ab_w_in[j]
            w_in = jnp.concatenate([w[:, :MIX_W], _dup_kv_cols(w[:, MIX_W:MIX_W + 128]), w[:, MIX_W + 128:]],
                                   axis=1).astype(BF16)
            q, kdup, vt, f = _premix_call(
                x_all, mod, row(norm_mix[i]), w_in, cos, sin, head_dup(a_q_norm[j], N_HEADS),
                head_dup(a_k_norm[j], 2 * KV_HEADS), seg, (MIX_W, KV_DUP_W, KV_W, B_IN),
                (BF16, BF16, BF16, BF16), True, bsz, ctx_tiles, tm)
            att = _gqa_call(q, kdup, vt, ctx_len)
            zl = jnp.zeros((B_LORA_W // 2, MIX_W), F32)
            dir_cat = lambda w: jnp.concatenate([jnp.concatenate([w[0], zl], axis=1),
                                                 jnp.concatenate([zl, w[1]], axis=1)], axis=0)
            w2cat = dir_cat(b_w2[j])
            a2cat = dir_cat(b_a2[j])
            p = dict(mu_prev=row(b_mu_prev[j]), mu_next=row(b_mu_next[j]), w0=row(b_w0[j]),
                     w2=w2cat.astype(BF16), a0=row(b_a0[j]), a2=a2cat.astype(BF16),
                     g2=b_g2[j].astype(BF16), k_k=row(b_k_k[j]), k_a=row(b_k_a[j]), r_k=row(b_r_k[j]),
                     seg=seg)
            r, v, kk, lw, bb, kd, bonus, g = _rwkv_prep_call(f, p, ctx_len, tm)
            yf, yb = _rwkv_scan_call(r, v, kk, lw, bb, kd, ctx_len)
            kind, extras, rows = "rwkv", [yf, yb, bonus, g], [row(b_ln_w[j]), row(b_ln_b[j]), seg]
        else:
            w = cd_w_in[j]
            w_in = jnp.concatenate([w[:, :MIX_W], _dup_kv_cols(w[:, MIX_W:MIX_W + 128]), w[:, MIX_W + 128:],
                                    jnp.zeros((D_MODEL, DT_PAD - 2 * N_HEADS), F32)], axis=1).astype(BF16)
            q, kdup, vt, z, xbc, dt_raw = _premix_call(
                x_all, mod, row(norm_mix[i]), w_in, cos, sin, jnp.ones((1, MIX_W), F32),
                jnp.ones((1, KV_DUP_W), F32), seg, (MIX_W, KV_DUP_W, KV_W, MIX_W, D_XBC, DT_PAD),
                (BF16, BF16, BF16, BF16, BF16, F32), False, bsz, ctx_tiles, tm)
            att = _swa_call(c_sink[j].astype(F32), q, kdup, vt, ctx_len, last)
            u = _conv_call(xbc, d_conv_w[j], row(d_conv_b[j]), ctx_len, tm)
            pad = jnp.zeros((DT_PAD - 2 * N_HEADS,), F32)
            dtb = jnp.concatenate([d_dt_bias[j].reshape(-1), pad]).reshape(1, -1)
            a_row = jnp.concatenate([-jnp.exp(d_A_log[j].astype(F32)).reshape(-1), pad]).reshape(1, -1)
            sel = np.zeros((2, DT_PAD, DT_PAD), np.float32)
            expand = np.zeros((DT_PAD, MIX_W), np.float32)
            for h in range(N_HEADS):
                sel[0, h, h] = 1.0
                sel[1, N_HEADS + h, h] = 1.0
                expand[h, HEAD_DIM * h:HEAD_DIM * (h + 1)] = 1.0
            yf, yb = _ssd_scan_call(u, dt_raw, dtb, a_row, jnp.asarray(sel, BF16), jnp.asarray(expand, BF16), ctx_len)
            dsk = jnp.repeat(d_D[j].astype(F32), HEAD_DIM).reshape(1, -1)
            kind, extras, rows = "ssd", [yf, yb, u, z], [dsk, row(d_norm_w[j])]
        x_all = _tail_call(kind, x_all, mod, att, extras, rows, w_out[i].astype(BF16), row(norm_mlp[i]),
                           mlp_w1[i].astype(BF16), mlp_w2[i].astype(BF16), row(final_norm),
                           bsz, ctx_len, tm, last, last)
    return x_all


def kernel(x, c, ctx, c_ctx, w_mod, b_mod, norm_mix, norm_mlp, w_out, mlp_w1, mlp_w2, final_norm, ab_w_in, a_q_norm, a_k_norm, b_mu_prev, b_mu_next, b_w0, b_w2, b_a0, b_a2, b_g2, b_k_k, b_k_a, b_r_k, b_ln_w, b_ln_b, cd_w_in, c_sink, d_conv_w, d_conv_b, d_dt_bias, d_A_log, d_D, d_norm_w):
    return _forward(x, c, ctx, c_ctx, w_mod, b_mod, norm_mix, norm_mlp, w_out, mlp_w1, mlp_w2, final_norm,
                    ab_w_in, a_q_norm, a_k_norm, b_mu_prev, b_mu_next, b_w0, b_w2, b_a0, b_a2, b_g2,
                    b_k_k, b_k_a, b_r_k, b_ln_w, b_ln_b,
                    cd_w_in, c_sink, d_conv_w, d_conv_b, d_dt_bias, d_A_log, d_D, d_norm_w)
```
